```python
import math
import jax
import jax.numpy as jnp
from jax import lax
import numpy as np


D_MODEL = 4096
BATCH = 4
SEQ = 2048
DEPTH = 4
DEC_BATCH = 2
DEC_SEQ = 4096
PAST_LEN = 128

HEAD_DIM = 128
N_HEADS_A = D_MODEL // 2 // HEAD_DIM
N_KV_A = N_HEADS_A // 4
GQA_GROUP = N_HEADS_A // N_KV_A
WINDOW = 128
BLOCK = 128
N_HEADS_B = D_MODEL // 4 // HEAD_DIM
GRID_W = 64
NA_ROWS = 8
NA_COLS = 16
N_HEADS_C = D_MODEL // 4 // HEAD_DIM
DIFF_QK_DIM = HEAD_DIM // 2
T5_BUCKETS = 32
T5_MAX_DIST = 128
N_T5_HEADS = N_HEADS_A + N_HEADS_C
N_EXPERTS = 16
EC_CAPACITY_FACTOR = 2
D_EXPERT = D_MODEL // 2
LN_EPS = 1e-5
NEG_INF = -1e30
ALPHA = (2 * DEPTH) ** 0.25
BETA = (8 * DEPTH) ** -0.25

A_Q_COLS = N_HEADS_A * HEAD_DIM
A_KV_COLS = N_KV_A * HEAD_DIM
B_COLS = N_HEADS_B * HEAD_DIM
C_COLS = N_HEADS_C * HEAD_DIM
IN_SPLITS = (A_Q_COLS, A_KV_COLS, A_KV_COLS, B_COLS, B_COLS, B_COLS, C_COLS, C_COLS, C_COLS)
IN_COLS = sum(IN_SPLITS)
MIX_WIDTH = A_Q_COLS + B_COLS + C_COLS
V_SLOTS = (2, 5, 8)

kernel_name = 'hybrid_parallel_heads_ec_encoder'


def layer_norm(x, g, b):
    xf = x.astype(jnp.float32)
    mu = jnp.mean(xf, axis=-1, keepdims=True)
    var = jnp.mean(jnp.square(xf - mu), axis=-1, keepdims=True)
    return ((xf - mu) * lax.rsqrt(var + LN_EPS) * g + b).astype(x.dtype)


def t5_bucket(rel):
    half = T5_BUCKETS // 2
    max_exact = half // 2
    n = jnp.abs(rel)
    n_f = jnp.maximum(n, 1).astype(jnp.float32)
    large = max_exact + (jnp.log(n_f / max_exact) / math.log(T5_MAX_DIST / max_exact) * (half - max_exact)).astype(jnp.int32)
    large = jnp.minimum(large, half - 1)
    return jnp.where(rel > 0, half, 0) + jnp.where(n < max_exact, n, large)


def t5_bias(table, rel):
    return jnp.transpose(table[t5_bucket(rel)], (2, 0, 1)).astype(jnp.float32)


def band_blocks(a, nb):
    bsz = a.shape[0]
    ab = a.reshape((bsz, nb, BLOCK) + a.shape[2:])
    ap = jnp.pad(ab, ((0, 0), (1, 1), (0, 0), (0, 0), (0, 0)))
    return jnp.concatenate([ap[:, :-2], ap[:, 1:-1], ap[:, 2:]], axis=2)


def windowed_gqa(q, k, v, sink, t5_tab_a):
    bsz, t = q.shape[:2]
    nb = t // BLOCK
    qb = q.reshape(bsz, nb, BLOCK, N_KV_A, GQA_GROUP, HEAD_DIM)
    kb = band_blocks(k, nb)
    vb = band_blocks(v, nb)
    s = jnp.einsum('bnqhgd,bnkhd->bnhgqk', qb, kb).astype(jnp.float32) * HEAD_DIM ** -0.5
    rel = jnp.arange(3 * BLOCK)[None, :] - BLOCK - jnp.arange(BLOCK)[:, None]
    s = s + t5_bias(t5_tab_a, rel).reshape(N_KV_A, GQA_GROUP, BLOCK, 3 * BLOCK)
    kpos = (jnp.arange(nb) * BLOCK - BLOCK)[:, None] + jnp.arange(3 * BLOCK)
    valid = (jnp.abs(rel) <= WINDOW)[None] & ((kpos >= 0) & (kpos < t))[:, None, :]
    s = jnp.where(valid[None, :, None, None], s, NEG_INF)
    sk = sink.astype(jnp.float32).reshape(N_KV_A, GQA_GROUP)[:, :, None, None]
    m = jnp.maximum(jnp.max(s, axis=-1, keepdims=True), sk)
    p = jnp.exp(s - m)
    p = p / (jnp.sum(p, axis=-1, keepdims=True) + jnp.exp(sk - m))
    o = jnp.einsum('bnhgqk,bnkhd->bnqhgd', p.astype(v.dtype), vb)
    return o.reshape(bsz, t, N_HEADS_A * HEAD_DIM)


def neighbourhood_attention_2d(q, k, v, rpb):
    bsz, t = q.shape[:2]
    rows = t // GRID_W
    kh = min(NA_ROWS, rows)
    r = jnp.arange(rows)
    row_idx = jnp.clip(r - kh // 2, 0, rows - kh)[:, None] + jnp.arange(kh)
    c = jnp.arange(GRID_W)
    col_start = jnp.clip(c - NA_COLS // 2, 0, GRID_W - NA_COLS)
    col_valid = (c[None, :] >= col_start[:, None]) & (c[None, :] < col_start[:, None] + NA_COLS)
    dr = row_idx - r[:, None]
    dc = jnp.clip(c[None, :] - c[:, None], -(NA_COLS - 1), NA_COLS - 1)
    qg = q.reshape(bsz, rows, GRID_W, N_HEADS_B, HEAD_DIM)
    kg = k.reshape(bsz, rows, GRID_W, N_HEADS_B, HEAD_DIM)[:, row_idx].reshape(bsz, rows, kh * GRID_W, N_HEADS_B, HEAD_DIM)
    vg = v.reshape(bsz, rows, GRID_W, N_HEADS_B, HEAD_DIM)[:, row_idx].reshape(bsz, rows, kh * GRID_W, N_HEADS_B, HEAD_DIM)
    s = jnp.einsum('brqhd,brkhd->brhqk', qg, kg).astype(jnp.float32) * HEAD_DIM ** -0.5
    bias = rpb[:, dr[:, None, :, None] + NA_ROWS - 1, dc[None, :, None, :] + NA_COLS - 1]
    bias = jnp.transpose(bias.reshape(N_HEADS_B, rows, GRID_W, kh * GRID_W), (1, 0, 2, 3)).astype(jnp.float32)
    s = s + bias[None]
    s = jnp.where(jnp.tile(col_valid, (1, kh))[None, None, None], s, NEG_INF)
    p = jax.nn.softmax(s, axis=-1)
    o = jnp.einsum('brhqk,brkhd->brqhd', p.astype(v.dtype), vg)
    return o.reshape(bsz, t, N_HEADS_B * HEAD_DIM)


def diff_attention(q, k, v, lam_vecs, lam_init, subln_g, t5_tab_c):
    bsz, t = q.shape[:2]
    nb = t // BLOCK
    lf = lam_vecs.astype(jnp.float32)
    lam = jnp.exp(jnp.sum(lf[0] * lf[1])) - jnp.exp(jnp.sum(lf[2] * lf[3])) + lam_init
    q_blocks = jnp.moveaxis(q.reshape(bsz, nb, BLOCK, N_HEADS_C, 2, DIFF_QK_DIM), 1, 0)
    kpos = jnp.arange(t)

    def one_block(args):
        qb, j = args
        s = jnp.einsum('bqhcd,bkhcd->bhcqk', qb, k).astype(jnp.float32) * DIFF_QK_DIM ** -0.5
        rel = kpos[None, :] - (j * BLOCK + jnp.arange(BLOCK))[:, None]
        s = s + t5_bias(t5_tab_c, rel)[None, :, None]
        p = jax.nn.softmax(s, axis=-1)
        w = p[:, :, 0] - lam * p[:, :, 1]
        return jnp.einsum('bhqk,bkhd->bqhd', w.astype(v.dtype), v)

    o = lax.map(one_block, (q_blocks, jnp.arange(nb)))
    o = jnp.moveaxis(o, 0, 1).reshape(bsz, t, N_HEADS_C, HEAD_DIM).astype(jnp.float32)
    o = o * lax.rsqrt(jnp.mean(jnp.square(o), axis=-1, keepdims=True) + LN_EPS) * subln_g * (1.0 - lam_init)
    return o.reshape(bsz, t, N_HEADS_C * HEAD_DIM).astype(v.dtype)


def expert_choice_ffn(x, w_router, w_gate, w_up, w_down):
    bsz, t, d = x.shape
    n = bsz * t
    cap = EC_CAPACITY_FACTOR * n // N_EXPERTS
    xt = x.reshape(n, d)
    aff = jax.nn.softmax(jnp.einsum('nd,de->ne', xt, w_router).astype(jnp.float32), axis=-1)
    gate, idx = lax.top_k(aff.T, cap)
    xe = xt[idx]
    hdn = jax.nn.silu(jnp.einsum('ecd,edf->ecf', xe, w_gate)) * jnp.einsum('ecd,edf->ecf', xe, w_up)
    ye = jnp.einsum('ecf,efd->ecd', hdn, w_down) * gate[..., None].astype(x.dtype)
    out = jnp.zeros_like(xt).at[idx.reshape(-1)].add(ye.reshape(-1, d))
    return out.reshape(bsz, t, d)


def encoder_layer(x, lam_init, w_in, w_out, ln1_g, ln1_b, ln2_g, ln2_b, sink_a, rpb_b, diff_lambda, subln_c, t5_table, w_router, w_gate, w_up, w_down):
    bsz, t, _ = x.shape
    h = jnp.einsum('btd,dc->btc', x, w_in)
    offs = np.cumsum(IN_SPLITS)[:-1].tolist()
    qa, ka, va, qb, kb, vb, qc, kc, vc = jnp.split(h, offs, axis=-1)
    oa = windowed_gqa(qa.reshape(bsz, t, N_HEADS_A, HEAD_DIM), ka.reshape(bsz, t, N_KV_A, HEAD_DIM),
                      va.reshape(bsz, t, N_KV_A, HEAD_DIM), sink_a, t5_table[:, :N_HEADS_A])
    ob = neighbourhood_attention_2d(qb.reshape(bsz, t, N_HEADS_B, HEAD_DIM), kb.reshape(bsz, t, N_HEADS_B, HEAD_DIM),
                                    vb.reshape(bsz, t, N_HEADS_B, HEAD_DIM), rpb_b)
    oc = diff_attention(qc.reshape(bsz, t, N_HEADS_C, 2, DIFF_QK_DIM), kc.reshape(bsz, t, N_HEADS_C, 2, DIFF_QK_DIM),
                        vc.reshape(bsz, t, N_HEADS_C, HEAD_DIM), diff_lambda, lam_init, subln_c, t5_table[:, N_HEADS_A:])
    mix = jnp.concatenate([oa, ob, oc], axis=-1)
    x = layer_norm(ALPHA * x + jnp.einsum('btc,cd->btd', mix, w_out), ln1_g, ln1_b)
    x = layer_norm(ALPHA * x + expert_choice_ffn(x, w_router, w_gate, w_up, w_down), ln2_g, ln2_b)
    return x


def trunk(x, w_in, w_out, ln1_g, ln1_b, ln2_g, ln2_b, sink_a, rpb_b, diff_lambda, subln_c, t5_table, w_router, w_gate, w_up, w_down):
    for l in range(DEPTH):
        lam_init = 0.8 - 0.6 * math.exp(-0.3 * l)
        x = encoder_layer(x, lam_init, w_in[l], w_out[l], ln1_g[l], ln1_b[l], ln2_g[l], ln2_b[l], sink_a[l], rpb_b[l],
                          diff_lambda[l], subln_c[l], t5_table, w_router[l], w_gate[l], w_up[l], w_down[l])
    return x


def setup_inputs(seed: int = 0) -> dict:
    key = jax.random.key(seed)
    ks = jax.random.split(key, 17)

    def nrm(k, shape, s):
        return jax.random.normal(k, shape, jnp.float32) * s

    col_scale = np.ones((IN_COLS,), np.float32)
    bounds = np.concatenate([[0], np.cumsum(IN_SPLITS)])
    for slot in V_SLOTS:
        col_scale[bounds[slot]:bounds[slot + 1]] = BETA
    return {
        'x_prompt': nrm(ks[0], (BATCH, SEQ, D_MODEL), 1.0),
        'x_sample': nrm(ks[1], (DEC_BATCH, DEC_SEQ, D_MODEL), 1.0),
        'w_in': nrm(ks[2], (DEPTH, D_MODEL, IN_COLS), D_MODEL ** -0.5) * jnp.asarray(col_scale),
        'w_out': nrm(ks[3], (DEPTH, MIX_WIDTH, D_MODEL), MIX_WIDTH ** -0.5 * BETA),
        'ln1_g': 1.0 + nrm(ks[4], (DEPTH, D_MODEL), 0.02),
        'ln1_b': nrm(ks[5], (DEPTH, D_MODEL), 0.02),
        'ln2_g': 1.0 + nrm(ks[6], (DEPTH, D_MODEL), 0.02),
        'ln2_b': nrm(ks[7], (DEPTH, D_MODEL), 0.02),
        'sink_a': nrm(ks[8], (DEPTH, N_HEADS_A), 0.5),
        'rpb_b': nrm(ks[9], (DEPTH, N_HEADS_B, 2 * NA_ROWS - 1, 2 * NA_COLS - 1), 0.1),
        'diff_lambda': nrm(ks[10], (DEPTH, 4, DIFF_QK_DIM), 0.1),
        'subln_c': 1.0 + nrm(ks[11], (DEPTH, HEAD_DIM), 0.02),
        't5_table': nrm(ks[12], (T5_BUCKETS, N_T5_HEADS), 0.2),
        'w_router': nrm(ks[13], (DEPTH, D_MODEL, N_EXPERTS), D_MODEL ** -0.5),
        'w_gate': nrm(ks[14], (DEPTH, N_EXPERTS, D_MODEL, D_EXPERT), D_MODEL ** -0.5),
        'w_up': nrm(ks[15], (DEPTH, N_EXPERTS, D_MODEL, D_EXPERT), D_MODEL ** -0.5 * BETA),
        'w_down': nrm(ks[16], (DEPTH, N_EXPERTS, D_EXPERT, D_MODEL), D_EXPERT ** -0.5 * BETA),
    }


def reference(x_prompt, x_sample, w_in, w_out, ln1_g, ln1_b, ln2_g, ln2_b, sink_a, rpb_b, diff_lambda, subln_c, t5_table, w_router, w_gate, w_up, w_down):
    y_prompt = trunk(x_prompt, w_in, w_out, ln1_g, ln1_b, ln2_g, ln2_b, sink_a, rpb_b, diff_lambda, subln_c, t5_table, w_router, w_gate, w_up, w_down)
    y_sample = trunk(x_sample, w_in, w_out, ln1_g, ln1_b, ln2_g, ln2_b, sink_a, rpb_b, diff_lambda, subln_c, t5_table, w_router, w_gate, w_up, w_down)
    return (y_prompt, y_sample)
```

```python
import functools
import math

import numpy as np
import jax
import jax.numpy as jnp
from jax import lax
from jax.experimental import pallas as pl
from jax.experimental.pallas import tpu as pltpu

F32 = jnp.float32
BF16 = jnp.bfloat16
I32 = jnp.int32

D_MODEL = 4096
DEPTH = 4
HEAD_DIM = 128
N_HEADS_A = 16
N_KV_A = 4
GQA_GROUP = 4
WINDOW = 128
BLOCK = 128
N_HEADS_B = 8
GRID_W = 64
NA_ROWS = 8
NA_COLS = 16
N_HEADS_C = 8
DIFF_QK_DIM = 64
T5_BUCKETS = 32
T5_MAX_DIST = 128
N_EXPERTS = 16
EC_CAPACITY_FACTOR = 2
D_EXPERT = 2048
LN_EPS = 1e-5
NEG_INF = -1e30
ALPHA = (2 * DEPTH) ** 0.25
IN_COLS = 9216

QA_CB, KA_CB, VA_CB = 0, 16, 20
QB_CB, KB_CB, VB_CB = 24, 32, 40
QC_CB, KC_CB, VC_CB = 48, 56, 64

LANES = 128
EXT_COLS = D_MODEL + LANES
VMEM_LIMIT = 56 * 1024 * 1024

QT = 256
KWIN = 768
GATHER_ROWS = 256


def _cparams(sem):
    return pltpu.CompilerParams(dimension_semantics=sem, vmem_limit_bytes=VMEM_LIMIT)


def _mm_kernel(x_ref, w_ref, o_ref):
    o_ref[...] = jnp.dot(x_ref[...], w_ref[...], preferred_element_type=F32).astype(o_ref.dtype)


def _in_proj(xb, w):
    n, d = xb.shape
    cols = w.shape[1]
    tm, tn = 1024, 1024
    return pl.pallas_call(
        _mm_kernel,
        grid=(n // tm, cols // tn),
        in_specs=[pl.BlockSpec((tm, d), lambda i, j: (i, 0)),
                  pl.BlockSpec((d, tn), lambda i, j: (0, j))],
        out_specs=pl.BlockSpec((tm, tn), lambda i, j: (i, j)),
        out_shape=jax.ShapeDtypeStruct((n, cols), BF16),
        compiler_params=_cparams(("parallel", "arbitrary")),
        name="in_proj",
    )(xb, w)


def _t5_bucket(rel):
    half = T5_BUCKETS // 2
    max_exact = half // 2
    n = jnp.abs(rel)
    n_f = jnp.maximum(n, 1).astype(F32)
    large = max_exact + (jnp.log(n_f / max_exact) / math.log(T5_MAX_DIST / max_exact) * (half - max_exact)).astype(I32)
    large = jnp.minimum(large, half - 1)
    return jnp.where(rel > 0, half, 0) + jnp.where(n < max_exact, n, large)


def _table_a(t5_tab_a):
    rel = jnp.arange(3 * BLOCK)[None, :] - BLOCK - jnp.arange(BLOCK)[:, None]
    bias = jnp.transpose(t5_tab_a[_t5_bucket(rel)], (2, 0, 1)).astype(F32)
    bias = jnp.where((jnp.abs(rel) <= WINDOW)[None], bias, NEG_INF)
    return bias.reshape(N_KV_A, GQA_GROUP * BLOCK, 3 * BLOCK)


def _table_c(t5_tab_c):
    qq = jnp.arange(QT)[:, None]
    kk = jnp.arange(KWIN)[None, :]
    tabs = []
    for v in range(3):
        rel = kk - qq - QT * v
        tabs.append(jnp.transpose(t5_tab_c[_t5_bucket(rel)], (2, 0, 1)).astype(F32))
    far = t5_tab_c[_t5_bucket(jnp.array([-(KWIN + QT), KWIN + QT]))].astype(F32)
    return jnp.stack(tabs, axis=1), far


def _table_b_index():
    rows_q, rows_k = QT // GRID_W, KWIN // GRID_W
    ql = np.arange(rows_q)[:, None, None, None]
    qc = np.arange(GRID_W)[None, :, None, None]
    kl = np.arange(rows_k)[None, None, :, None]
    kc = np.arange(GRID_W)[None, None, None, :]
    col_start = np.clip(qc - NA_COLS // 2, 0, GRID_W - NA_COLS)
    col_valid = (kc >= col_start) & (kc < col_start + NA_COLS)
    dc = np.clip(kc - qc, -(NA_COLS - 1), NA_COLS - 1) + NA_COLS - 1
    drs, dcs, valids = [], [], []
    for ty in range(3):
        off = rows_q * ty
        dr = kl - off - ql
        if ty == 0:
            win = (kl >= 0) & (kl < NA_ROWS)
        elif ty == 1:
            win = (kl >= ql) & (kl < ql + NA_ROWS)
        else:
            win = (kl >= rows_k - NA_ROWS) & (kl < rows_k)
        valid = np.broadcast_to(win & col_valid, (rows_q, GRID_W, rows_k, GRID_W))
        drs.append(np.broadcast_to(np.clip(dr + NA_ROWS - 1, 0, 2 * NA_ROWS - 2), valid.shape).reshape(QT, KWIN))
        dcs.append(np.broadcast_to(dc, valid.shape).reshape(QT, KWIN))
        valids.append(valid.reshape(QT, KWIN))
    return np.stack(drs), np.stack(dcs), np.stack(valids)


_TB_DR, _TB_DC, _TB_VALID = _table_b_index()


def _table_b(rpb):
    bias = rpb[:, _TB_DR, _TB_DC].astype(F32)
    return jnp.where(_TB_VALID[None], bias, NEG_INF)


def _attn_a_kernel(q_ref, kp_ref, kc_ref, kn_ref, vp_ref, vc_ref, vn_ref, tab_ref, sink_ref, o_ref, *, nb):
    n = pl.program_id(1)
    q = q_ref[...]
    q4 = jnp.concatenate([q[:, i * HEAD_DIM:(i + 1) * HEAD_DIM] for i in range(GQA_GROUP)], axis=0)
    k3 = jnp.concatenate([kp_ref[...], kc_ref[...], kn_ref[...]], axis=0)
    v3 = jnp.concatenate([vp_ref[...], vc_ref[...], vn_ref[...]], axis=0)
    s = lax.dot_general(q4, k3, (((1,), (1,)), ((), ())), preferred_element_type=F32)
    s = s * (HEAD_DIM ** -0.5) + tab_ref[0]
    col = lax.broadcasted_iota(I32, (1, 3 * BLOCK), 1)
    lo = jnp.where(n == 0, BLOCK, 0)
    hi = jnp.where(n == nb - 1, 2 * BLOCK, 3 * BLOCK)
    s = jnp.where((col < lo) | (col >= hi), NEG_INF, s)
    sk = sink_ref[0]
    m = jnp.maximum(jnp.max(s, axis=-1, keepdims=True), sk)
    p = jnp.exp(s - m)
    l = jnp.sum(p, axis=-1, keepdims=True) + jnp.exp(sk - m)
    o = jnp.dot(p.astype(BF16), v3, preferred_element_type=F32) / l
    o_ref[...] = jnp.concatenate([o[i * BLOCK:(i + 1) * BLOCK] for i in range(GQA_GROUP)], axis=1).astype(o_ref.dtype)


def _attn_a(h, tab_a, sink_col, bsz, t):
    nb = t // BLOCK
    n = bsz * t

    def kv_spec(cb, shift):
        return pl.BlockSpec((BLOCK, HEAD_DIM), lambda b, i, g: (b * nb + jnp.clip(i + shift, 0, nb - 1), cb + g))

    return pl.pallas_call(
        functools.partial(_attn_a_kernel, nb=nb),
        grid=(bsz, nb, N_KV_A),
        in_specs=[pl.BlockSpec((BLOCK, GQA_GROUP * HEAD_DIM), lambda b, i, g: (b * nb + i, g)),
                  kv_spec(KA_CB, -1), kv_spec(KA_CB, 0), kv_spec(KA_CB, 1),
                  kv_spec(VA_CB, -1), kv_spec(VA_CB, 0), kv_spec(VA_CB, 1),
                  pl.BlockSpec((1, GQA_GROUP * BLOCK, 3 * BLOCK), lambda b, i, g: (g, 0, 0)),
                  pl.BlockSpec((1, GQA_GROUP * BLOCK, 1), lambda b, i, g: (g, 0, 0))],
        out_specs=pl.BlockSpec((BLOCK, GQA_GROUP * HEAD_DIM), lambda b, i, g: (b * nb + i, g)),
        out_shape=jax.ShapeDtypeStruct((n, N_HEADS_A * HEAD_DIM), BF16),
        compiler_params=_cparams(("parallel", "parallel", "arbitrary")),
        name="attn_window",
    )(h, h, h, h, h, h, h, tab_a, sink_col)


def _attn_b_kernel(q_ref, k0_ref, k1_ref, k2_ref, v0_ref, v1_ref, v2_ref, tab_ref, o_ref):
    q = q_ref[...]
    k3 = jnp.concatenate([k0_ref[...], k1_ref[...], k2_ref[...]], axis=0)
    v3 = jnp.concatenate([v0_ref[...], v1_ref[...], v2_ref[...]], axis=0)
    s = lax.dot_general(q, k3, (((1,), (1,)), ((), ())), preferred_element_type=F32)
    s = s * (HEAD_DIM ** -0.5) + tab_ref[0, 0]
    m = jnp.max(s, axis=-1, keepdims=True)
    p = jnp.exp(s - m)
    l = jnp.sum(p, axis=-1, keepdims=True)
    o = jnp.dot(p.astype(BF16), v3, preferred_element_type=F32) / l
    o_ref[...] = o.astype(o_ref.dtype)


def _attn_b(h, tab_b, bsz, t):
    nq = t // QT
    n = bsz * t

    def kv_spec(cb, j):
        return pl.BlockSpec((QT, HEAD_DIM), lambda b, g, hd: (b * nq + jnp.clip(g - 1, 0, nq - 3) + j, cb + hd))

    def tab_map(b, g, hd):
        ty = jnp.where(g == 0, 0, jnp.where(g == nq - 1, 2, 1))
        return (hd, ty, 0, 0)

    return pl.pallas_call(
        _attn_b_kernel,
        grid=(bsz, nq, N_HEADS_B),
        in_specs=[pl.BlockSpec((QT, HEAD_DIM), lambda b, g, hd: (b * nq + g, QB_CB + hd)),
                  kv_spec(KB_CB, 0), kv_spec(KB_CB, 1), kv_spec(KB_CB, 2),
                  kv_spec(VB_CB, 0), kv_spec(VB_CB, 1), kv_spec(VB_CB, 2),
                  pl.BlockSpec((1, 1, QT, KWIN), tab_map)],
        out_specs=pl.BlockSpec((QT, HEAD_DIM), lambda b, g, hd: (b * nq + g, hd)),
        out_shape=jax.ShapeDtypeStruct((n, N_HEADS_B * HEAD_DIM), BF16),
        compiler_params=_cparams(("parallel", "parallel", "arbitrary")),
        name="attn_nbr",
    )(h, h, h, h, h, h, h, tab_b)


def _attn_c_kernel(scal_ref, q_ref, k_ref, v_ref, tab_ref, g_ref, o_ref,
                   m_ref, l_ref, acc_ref, *, t, lam_init):
    hd = pl.program_id(1)
    i = pl.program_id(2)
    q = q_ref[...] * jnp.asarray(DIFF_QK_DIM ** -0.5, BF16)
    qs = (q[:, :DIFF_QK_DIM], q[:, DIFF_QK_DIM:])
    m_ref[...] = jnp.full(m_ref.shape, NEG_INF, F32)
    l_ref[...] = jnp.zeros(l_ref.shape, F32)
    acc_ref[...] = jnp.zeros(acc_ref.shape, F32)

    def tile(start, width, bias):
        kt = k_ref[pl.ds(start, width), :]
        vt = v_ref[pl.ds(start, width), :]
        ks = (kt[:, :DIFF_QK_DIM], kt[:, DIFF_QK_DIM:])
        for c in range(2):
            s = lax.dot_general(qs[c], ks[c], (((1,), (1,)), ((), ())), preferred_element_type=F32) + bias
            m_old = m_ref[c]
            m_new = jnp.maximum(m_old, jnp.max(s, axis=-1, keepdims=True))
            a = jnp.exp(m_old - m_new)
            p = jnp.exp(s - m_new)
            l_ref[c] = a * l_ref[c] + jnp.sum(p, axis=-1, keepdims=True)
            acc_ref[c] = a * acc_ref[c] + jnp.dot(p.astype(BF16), vt, preferred_element_type=F32)
            m_ref[c] = m_new

    kstart = pl.multiple_of(jnp.clip(i * QT - QT, 0, t - KWIN), QT)
    variant = (i * QT - kstart) // QT
    tile(kstart, KWIN, tab_ref[0, variant])

    far_neg = scal_ref[1 + 2 * hd]
    far_pos = scal_ref[2 + 2 * hd]

    def left(j, carry):
        tile(pl.multiple_of(j * QT, QT), QT, far_neg)
        return carry

    def right(j, carry):
        tile(pl.multiple_of(kstart + KWIN + j * QT, QT), QT, far_pos)
        return carry

    lax.fori_loop(0, kstart // QT, left, 0)
    lax.fori_loop(0, (t - KWIN - kstart) // QT, right, 0)

    lam = scal_ref[0]
    o = acc_ref[0] / l_ref[0] - lam * (acc_ref[1] / l_ref[1])
    o = o * lax.rsqrt(jnp.mean(jnp.square(o), axis=-1, keepdims=True) + LN_EPS) * g_ref[...] * (1.0 - lam_init)
    o_ref[...] = o.astype(o_ref.dtype)


def _attn_c(h, tab_c, scal, subln_g, bsz, t, lam_init):
    nq = t // QT
    n = bsz * t
    grid_spec = pltpu.PrefetchScalarGridSpec(
        num_scalar_prefetch=1,
        grid=(bsz, N_HEADS_C, nq),
        in_specs=[pl.BlockSpec((QT, HEAD_DIM), lambda b, hd, i, s: (b * nq + i, QC_CB + hd)),
                  pl.BlockSpec((t, HEAD_DIM), lambda b, hd, i, s: (b, KC_CB + hd)),
                  pl.BlockSpec((t, HEAD_DIM), lambda b, hd, i, s: (b, VC_CB + hd)),
                  pl.BlockSpec((1, 3, QT, KWIN), lambda b, hd, i, s: (hd, 0, 0, 0)),
                  pl.BlockSpec((1, HEAD_DIM), lambda b, hd, i, s: (0, 0))],
        out_specs=pl.BlockSpec((QT, HEAD_DIM), lambda b, hd, i, s: (b * nq + i, hd)),
        scratch_shapes=[pltpu.VMEM((2, QT, 1), F32), pltpu.VMEM((2, QT, 1), F32),
                        pltpu.VMEM((2, QT, HEAD_DIM), F32)],
    )
    return pl.pallas_call(
        functools.partial(_attn_c_kernel, t=t, lam_init=lam_init),
        grid_spec=grid_spec,
        out_shape=jax.ShapeDtypeStruct((n, N_HEADS_C * HEAD_DIM), BF16),
        compiler_params=_cparams(("parallel", "parallel", "arbitrary")),
        name="attn_diff",
    )(scal, h, h, h, tab_c, subln_g)


def _split_bf16(x):
    hi = x.astype(BF16)
    lo = (x - hi.astype(F32)).astype(BF16)
    return hi, lo


def _out_ln_router_kernel(oa_ref, ob_ref, oc_ref, w_ref, x_ref, g_ref, b_ref, wr_ref,
                          xe_ref, afft_ref, pre_ref, *, nj, tn):
    j = pl.program_id(1)
    ca, cb = oa_ref.shape[1], ob_ref.shape[1]
    acc = jnp.dot(oa_ref[...], w_ref[0:ca, :], preferred_element_type=F32)
    acc += jnp.dot(ob_ref[...], w_ref[ca:ca + cb, :], preferred_element_type=F32)
    acc += jnp.dot(oc_ref[...], w_ref[ca + cb:, :], preferred_element_type=F32)
    pre_ref[j] = ALPHA * x_ref[...] + acc

    @pl.when(j == nj - 1)
    def _():
        tm = pre_ref.shape[1]
        d = nj * tn
        tot = jnp.zeros((tm, 1), F32)
        for c in range(nj):
            tot += jnp.sum(pre_ref[c], axis=-1, keepdims=True)
        mu = tot / d
        sq = jnp.zeros((tm, 1), F32)
        for c in range(nj):
            sq += jnp.sum(jnp.square(pre_ref[c] - mu), axis=-1, keepdims=True)
        rstd = lax.rsqrt(sq / d + LN_EPS)
        logits = jnp.zeros((tm, LANES), F32)
        for c in range(nj):
            sl = slice(c * tn, (c + 1) * tn)
            y = (pre_ref[c] - mu) * rstd * g_ref[:, sl] + b_ref[:, sl]
            xe_ref[:, sl] = y
            yh, yl = _split_bf16(y)
            wh, wl = _split_bf16(wr_ref[sl, :])
            logits += jnp.dot(yh, wh, preferred_element_type=F32)
            logits += jnp.dot(yh, wl, preferred_element_type=F32)
            logits += jnp.dot(yl, wh, preferred_element_type=F32)
        lane = lax.broadcasted_iota(I32, (1, LANES), 1)
        logits = jnp.where(lane < N_EXPERTS, logits, NEG_INF)
        mx = jnp.max(logits, axis=-1, keepdims=True)
        ex = jnp.exp(logits - mx)
        aff = ex / jnp.sum(ex, axis=-1, keepdims=True)
        xe_ref[:, d:] = aff
        afft_ref[...] = aff.T[:N_EXPERTS, :]


def _out_ln_router(oa, ob, oc, w_out, x, g, b, w_router_pad):
    n, d = x.shape
    tm, tn = 512, 512
    nj = d // tn
    return pl.pallas_call(
        functools.partial(_out_ln_router_kernel, nj=nj, tn=tn),
        grid=(n // tm, nj),
        in_specs=[pl.BlockSpec((tm, oa.shape[1]), lambda i, j: (i, 0)),
                  pl.BlockSpec((tm, ob.shape[1]), lambda i, j: (i, 0)),
                  pl.BlockSpec((tm, oc.shape[1]), lambda i, j: (i, 0)),
                  pl.BlockSpec((d, tn), lambda i, j: (0, j)),
                  pl.BlockSpec((tm, tn), lambda i, j: (i, j)),
                  pl.BlockSpec((1, d), lambda i, j: (0, 0)),
                  pl.BlockSpec((1, d), lambda i, j: (0, 0)),
                  pl.BlockSpec((d, LANES), lambda i, j: (0, 0))],
        out_specs=[pl.BlockSpec((tm, d + LANES), lambda i, j: (i, 0)),
                   pl.BlockSpec((N_EXPERTS, tm), lambda i, j: (0, i))],
        out_shape=[jax.ShapeDtypeStruct((n, d + LANES), F32),
                   jax.ShapeDtypeStruct((N_EXPERTS, n), F32)],
        scratch_shapes=[pltpu.VMEM((nj, tm, tn), F32)],
        compiler_params=_cparams(("parallel", "arbitrary")),
        name="out_proj_ln_router",
    )(oa, ob, oc, w_out, x, g, b, w_router_pad)


def _route_kernel(aff_ref, idx_ref, incl_ref, incl8_ref, *, cap):
    e_cnt, n = aff_ref.shape
    nch = n // LANES
    bits = pltpu.bitcast(aff_ref[...], I32)

    def count(mask):
        return jnp.sum(jnp.where(mask, 1.0, 0.0), axis=1, keepdims=True)

    def search(it, v):
        cand = v | jnp.left_shift(jnp.int32(1), 30 - it)
        return jnp.where(count(bits >= cand) >= cap, cand, v)

    thr = lax.fori_loop(0, 31, search, jnp.zeros((e_cnt, 1), I32))
    gtf = jnp.where(bits > thr, 1.0, 0.0)
    eqf = jnp.where(bits == thr, 1.0, 0.0)
    need = cap - jnp.sum(gtf, axis=1, keepdims=True)

    r = lax.broadcasted_iota(I32, (LANES, LANES), 0)
    c = lax.broadcasted_iota(I32, (LANES, LANES), 1)
    tri = jnp.where(r <= c, 1.0, 0.0).astype(BF16)
    ones = jnp.ones((LANES, LANES), BF16)

    def prefix(maskf):
        carry = jnp.zeros((e_cnt, LANES), F32)
        for j in range(nch):
            sl = slice(j * LANES, (j + 1) * LANES)
            mj = maskf[:, sl].astype(BF16)
            incl_ref[:, sl] = jnp.dot(mj, tri, preferred_element_type=F32) + carry
            carry = carry + jnp.dot(mj, ones, preferred_element_type=F32)

    prefix(eqf)
    tie_rank = incl_ref[...] - eqf
    chosen = gtf + eqf * jnp.where(tie_rank < need, 1.0, 0.0)
    prefix(chosen)
    for e in range(e_cnt):
        incl8_ref[e] = jnp.broadcast_to(incl_ref[e:e + 1, :], (8, n))

    ones_row = jnp.ones((8, LANES), BF16)
    slot_iota = lax.broadcasted_iota(I32, (LANES, 1), 0)

    def compact(it, carry):
        e = it // (cap // LANES)
        sb = it % (cap // LANES)
        slot = (slot_iota + sb * LANES).astype(F32)
        acc = jnp.zeros((LANES, LANES), F32)
        for j in range(nch):
            blk = incl8_ref[e, :, j * LANES:(j + 1) * LANES]
            acc += jnp.where(jnp.concatenate([blk] * (LANES // 8), axis=0) <= slot, 1.0, 0.0)
        tot = lax.dot_general(ones_row, acc.astype(BF16), (((1,), (1,)), ((), ())), preferred_element_type=F32)
        idx_ref[e, sb] = tot.astype(I32)
        return carry

    lax.fori_loop(0, e_cnt * (cap // LANES), compact, 0)


def _route(aff_t, cap):
    e_cnt, n = aff_t.shape
    idx = pl.pallas_call(
        functools.partial(_route_kernel, cap=cap),
        out_shape=jax.ShapeDtypeStruct((e_cnt, cap // LANES, 8, LANES), I32),
        scratch_shapes=[pltpu.VMEM((e_cnt, n), F32), pltpu.VMEM((e_cnt, 8, n), F32)],
        compiler_params=pltpu.CompilerParams(vmem_limit_bytes=VMEM_LIMIT),
        name="route",
    )(aff_t)
    return idx[:, :, 0, :].reshape(e_cnt, 1, cap)


def _ffn_kernel(idx_ref, x_hbm, wg_ref, wu_ref, wd_ref, y_ref,
                xe_ref, h_ref, gate_ref, stage_ref, sem, *, cap, nf, tf, d):
    e = pl.program_id(0)
    s = pl.program_id(1)
    nchunk = cap // GATHER_ROWS

    def gather_copy(row, c, slot, r):
        return pltpu.make_async_copy(x_hbm.at[pl.ds(row, 1), :],
                                     stage_ref.at[slot, pl.ds(r, 1), :], sem.at[slot])

    def start_gather(c, slot):
        def body(r, carry):
            gather_copy(idx_ref[0, 0, c * GATHER_ROWS + r], c, slot, r).start()
            return carry
        lax.fori_loop(0, GATHER_ROWS, body, 0)

    def wait_gather(c, slot):
        def body(r, carry):
            gather_copy(0, c, slot, r).wait()
            return carry
        lax.fori_loop(0, GATHER_ROWS, body, 0)

    @pl.when(s == 0)
    def _():
        lane = lax.broadcasted_iota(I32, (1, LANES), 1)
        start_gather(0, 0)
        for c in range(nchunk):
            slot = c % 2
            if c + 1 < nchunk:
                start_gather(c + 1, 1 - slot)
            wait_gather(c, slot)
            rows = slice(c * GATHER_ROWS, (c + 1) * GATHER_ROWS)
            xe_ref[rows, :] = stage_ref[slot, :, 0:d].astype(BF16)
            aff = stage_ref[slot, :, d:d + LANES]
            gate_ref[rows, :] = jnp.sum(jnp.where(lane == e, aff, 0.0), axis=-1, keepdims=True)

    @pl.when(s < nf)
    def _():
        xe = xe_ref[...]
        g = jnp.dot(xe, wg_ref[0].astype(BF16), preferred_element_type=F32)
        u = jnp.dot(xe, wu_ref[0].astype(BF16), preferred_element_type=F32)
        h_ref[s] = (g * (1.0 / (1.0 + jnp.exp(-g))) * u).astype(BF16)

    @pl.when(s >= nf)
    def _():
        acc = jnp.dot(h_ref[0], wd_ref[0, 0:tf, :].astype(BF16), preferred_element_type=F32)
        for c in range(1, nf):
            acc += jnp.dot(h_ref[c], wd_ref[0, c * tf:(c + 1) * tf, :].astype(BF16), preferred_element_type=F32)
        y_ref[...] = acc * gate_ref[...]


def _expert_ffn(idx, x_ext, w_gate, w_up, w_down, cap):
    e_cnt, d, f = w_gate.shape
    tf, tn = 256, 512
    nf, nn = f // tf, d // tn
    return pl.pallas_call(
        functools.partial(_ffn_kernel, cap=cap, nf=nf, tf=tf, d=d),
        grid=(e_cnt, nf + nn),
        in_specs=[pl.BlockSpec((1, 1, cap), lambda e, s: (e, 0, 0), memory_space=pltpu.SMEM),
                  pl.BlockSpec(memory_space=pl.ANY),
                  pl.BlockSpec((1, d, tf), lambda e, s: (e, 0, jnp.minimum(s, nf - 1))),
                  pl.BlockSpec((1, d, tf), lambda e, s: (e, 0, jnp.minimum(s, nf - 1))),
                  pl.BlockSpec((1, f, tn), lambda e, s: (e, 0, jnp.maximum(s - nf, 0)))],
        out_specs=pl.BlockSpec((cap, tn), lambda e, s: (e, jnp.maximum(s - nf, 0))),
        out_shape=jax.ShapeDtypeStruct((e_cnt * cap, d), F32),
        scratch_shapes=[pltpu.VMEM((cap, d), BF16),
                        pltpu.VMEM((nf, cap, tf), BF16),
                        pltpu.VMEM((cap, 1), F32),
                        pltpu.VMEM((2, GATHER_ROWS, x_ext.shape[1]), F32),
                        pltpu.SemaphoreType.DMA((2,))],
        compiler_params=_cparams(("arbitrary", "arbitrary")),
        name="expert_ffn",
    )(idx, x_ext, w_gate, w_up, w_down)


def _combine_kernel(idx_ref, y_hbm, acc_in_hbm, acc_hbm, buf_ref, ybuf_ref, gsem, ssem, ysem, *, cap):
    del acc_in_hbm
    e = pl.program_id(0)
    nchunk = cap // GATHER_ROWS

    def y_copy(c, slot):
        return pltpu.make_async_copy(y_hbm.at[pl.ds(e * cap + c * GATHER_ROWS, GATHER_ROWS), :],
                                     ybuf_ref.at[slot], ysem.at[slot])

    def row_in(row, slot, r):
        return pltpu.make_async_copy(acc_hbm.at[pl.ds(row, 1), :], buf_ref.at[slot, pl.ds(r, 1), :], gsem.at[slot])

    def row_out(row, slot, r):
        return pltpu.make_async_copy(buf_ref.at[slot, pl.ds(r, 1), :], acc_hbm.at[pl.ds(row, 1), :], ssem.at[slot])

    def for_rows(fn):
        def body(r, carry):
            fn(r)
            return carry
        lax.fori_loop(0, GATHER_ROWS, body, 0)

    def start_gather(c, slot):
        y_copy(c, slot).start()
        for_rows(lambda r: row_in(idx_ref[0, 0, c * GATHER_ROWS + r], slot, r).start())

    def wait_gather(c, slot):
        y_copy(c, slot).wait()
        for_rows(lambda r: row_in(0, slot, r).wait())

    def start_scatter(c, slot):
        for_rows(lambda r: row_out(idx_ref[0, 0, c * GATHER_ROWS + r], slot, r).start())

    def wait_scatter(slot):
        for_rows(lambda r: row_out(0, slot, r).wait())

    start_gather(0, 0)
    for c in range(nchunk):
        slot = c % 2
        if c + 1 < nchunk:
            if c >= 1:
                wait_scatter(1 - slot)
            start_gather(c + 1, 1 - slot)
        wait_gather(c, slot)
        buf_ref[slot] = buf_ref[slot] + ybuf_ref[slot]
        start_scatter(c, slot)
    if nchunk >= 2:
        wait_scatter(nchunk % 2)
    wait_scatter((nchunk - 1) % 2)


def _combine(idx, y, acc, cap):
    e_cnt = idx.shape[0]
    n, d = acc.shape
    return pl.pallas_call(
        functools.partial(_combine_kernel, cap=cap),
        grid=(e_cnt,),
        in_specs=[pl.BlockSpec((1, 1, cap), lambda e: (e, 0, 0), memory_space=pltpu.SMEM),
                  pl.BlockSpec(memory_space=pl.ANY),
                  pl.BlockSpec(memory_space=pl.ANY)],
        out_specs=pl.BlockSpec(memory_space=pl.ANY),
        out_shape=jax.ShapeDtypeStruct((n, d), F32),
        input_output_aliases={2: 0},
        scratch_shapes=[pltpu.VMEM((2, GATHER_ROWS, d), F32),
                        pltpu.VMEM((2, GATHER_ROWS, d), F32),
                        pltpu.SemaphoreType.DMA((2,)),
                        pltpu.SemaphoreType.DMA((2,)),
                        pltpu.SemaphoreType.DMA((2,))],
        compiler_params=_cparams(("arbitrary",)),
        name="combine",
    )(idx, y, acc)


def _ln2_kernel(x_ref, s_ref, g_ref, b_ref, o_ref, ob_ref):
    pre = ALPHA * x_ref[...] + s_ref[...]
    mu = jnp.mean(pre, axis=-1, keepdims=True)
    var = jnp.mean(jnp.square(pre - mu), axis=-1, keepdims=True)
    y = (pre - mu) * lax.rsqrt(var + LN_EPS) * g_ref[...] + b_ref[...]
    o_ref[...] = y
    ob_ref[...] = y.astype(BF16)


def _ln2(x_ext, moe, g, b):
    n, d = moe.shape
    tm = 256
    return pl.pallas_call(
        _ln2_kernel,
        grid=(n // tm,),
        in_specs=[pl.BlockSpec((tm, d), lambda i: (i, 0)),
                  pl.BlockSpec((tm, d), lambda i: (i, 0)),
                  pl.BlockSpec((1, d), lambda i: (0, 0)),
                  pl.BlockSpec((1, d), lambda i: (0, 0))],
        out_specs=[pl.BlockSpec((tm, d), lambda i: (i, 0)),
                   pl.BlockSpec((tm, d), lambda i: (i, 0))],
        out_shape=[jax.ShapeDtypeStruct((n, d), F32), jax.ShapeDtypeStruct((n, d), BF16)],
        compiler_params=_cparams(("parallel",)),
        name="ln2",
    )(x_ext, moe, g, b)


def _layer(x, xb, bsz, t, lam_init, p):
    n = bsz * t
    cap = EC_CAPACITY_FACTOR * n // N_EXPERTS
    h = _in_proj(xb, p["w_in"])
    oa = _attn_a(h, p["tab_a"], p["sink_col"], bsz, t)
    ob = _attn_b(h, p["tab_b"], bsz, t)
    oc = _attn_c(h, p["tab_c"], p["scal_c"], p["subln"], bsz, t, lam_init)
    x_ext, aff_t = _out_ln_router(oa, ob, oc, p["w_out"], x, p["ln1_g"], p["ln1_b"], p["w_router"])
    idx = _route(aff_t, cap)
    y = _expert_ffn(idx, x_ext, p["w_gate"], p["w_up"], p["w_down"], cap)
    moe = _combine(idx, y, jnp.zeros((n, D_MODEL), F32), cap)
    return _ln2(x_ext, moe, p["ln2_g"], p["ln2_b"])


def _trunk(x3, layers):
    bsz, t, d = x3.shape
    x = x3.reshape(bsz * t, d)
    xb = x.astype(BF16)
    for l in range(DEPTH):
        lam_init = 0.8 - 0.6 * math.exp(-0.3 * l)
        x, xb = _layer(x, xb, bsz, t, lam_init, layers[l])
    return x.reshape(bsz, t, d)


def kernel(x_prompt, x_sample, w_in, w_out, ln1_g, ln1_b, ln2_g, ln2_b, sink_a, rpb_b, diff_lambda, subln_c,
           t5_table, w_router, w_gate, w_up, w_down):
    tab_a = _table_a(t5_table[:, :N_HEADS_A])
    tab_c, far_c = _table_c(t5_table[:, N_HEADS_A:])
    w_in_b = w_in.astype(BF16)
    w_out_b = w_out.astype(BF16)
    layers = []
    for l in range(DEPTH):
        lam_init = 0.8 - 0.6 * math.exp(-0.3 * l)
        lf = diff_lambda[l].astype(F32)
        lam = jnp.exp(jnp.sum(lf[0] * lf[1])) - jnp.exp(jnp.sum(lf[2] * lf[3])) + lam_init
        scal_c = jnp.concatenate([lam[None], far_c.T.reshape(-1)]).astype(F32)
        sink_col = jnp.repeat(sink_a[l].astype(F32), BLOCK).reshape(N_KV_A, GQA_GROUP * BLOCK, 1)
        layers.append(dict(
            w_in=w_in_b[l], w_out=w_out_b[l],
            ln1_g=ln1_g[l][None], ln1_b=ln1_b[l][None], ln2_g=ln2_g[l][None], ln2_b=ln2_b[l][None],
            tab_a=tab_a, sink_col=sink_col, tab_b=_table_b(rpb_b[l]), tab_c=tab_c, scal_c=scal_c,
            subln=subln_c[l][None].astype(F32),
            w_router=jnp.pad(w_router[l], ((0, 0), (0, LANES - N_EXPERTS))),
            w_gate=w_gate[l], w_up=w_up[l], w_down=w_down[l]))
    return (_trunk(x_prompt, layers), _trunk(x_sample, layers))
```

```python
import functools
import math

import numpy as np
import jax
import jax.numpy as jnp
from jax import lax
from jax.experimental import pallas as pl
from jax.experimental.pallas import tpu as pltpu

F32 = jnp.float32
BF16 = jnp.bfloat16
I32 = jnp.int32

D_MODEL = 4096
DEPTH = 4
HEAD_DIM = 128
N_HEADS_A = 16
N_KV_A = 4
GQA_GROUP = 4
WINDOW = 128
BLOCK = 128
N_HEADS_B = 8
GRID_W = 64
NA_ROWS = 8
NA_COLS = 16
N_HEADS_C = 8
DIFF_QK_DIM = 64
T5_BUCKETS = 32
T5_MAX_DIST = 128
N_EXPERTS = 16
EC_CAPACITY_FACTOR = 2
D_EXPERT = 2048
LN_EPS = 1e-5
NEG_INF = -1e30
ALPHA = (2 * DEPTH) ** 0.25
IN_COLS = 9216

QA_CB, KA_CB, VA_CB = 0, 16, 20
QB_CB, KB_CB, VB_CB = 24, 32, 40
QC_CB, KC_CB, VC_CB = 48, 56, 64

LANES = 128
EXT_COLS = D_MODEL + LANES
VMEM_LIMIT = 56 * 1024 * 1024

QT = 256
KWIN = 768
KT = 512
GATHER_ROWS = 256


def _cparams(sem):
    return pltpu.CompilerParams(dimension_semantics=sem, vmem_limit_bytes=VMEM_LIMIT)


def _mm_kernel(x_ref, w_ref, o_ref):
    o_ref[...] = jnp.dot(x_ref[...], w_ref[...], preferred_element_type=F32).astype(o_ref.dtype)


def _in_proj(xb, w, l):
    n, d = xb.shape
    cols = w.shape[2]
    tm, tn = 1024, 1024
    return pl.pallas_call(
        _mm_kernel,
        grid=(n // tm, cols // tn),
        in_specs=[pl.BlockSpec((tm, d), lambda i, j: (i, 0)),
                  pl.BlockSpec((None, d, tn), lambda i, j: (l, 0, j))],
        out_specs=pl.BlockSpec((tm, tn), lambda i, j: (i, j)),
        out_shape=jax.ShapeDtypeStruct((n, cols), BF16),
        compiler_params=_cparams(("parallel", "arbitrary")),
        name="in_proj",
    )(xb, w)


def _t5_bucket(rel):
    half = T5_BUCKETS // 2
    max_exact = half // 2
    n = jnp.abs(rel)
    n_f = jnp.maximum(n, 1).astype(F32)
    large = max_exact + (jnp.log(n_f / max_exact) / math.log(T5_MAX_DIST / max_exact) * (half - max_exact)).astype(I32)
    large = jnp.minimum(large, half - 1)
    return jnp.where(rel > 0, half, 0) + jnp.where(n < max_exact, n, large)


def _t5_lookup(table, rel):
    onehot = (_t5_bucket(rel)[..., None] == jnp.arange(T5_BUCKETS)).astype(F32)
    return jnp.einsum('...b,bh->h...', onehot, table.astype(F32), precision=lax.Precision.HIGHEST)


def _table_a(t5_tab_a):
    rel = jnp.arange(3 * BLOCK)[None, :] - BLOCK - jnp.arange(BLOCK)[:, None]
    bias = jnp.where((jnp.abs(rel) <= WINDOW)[None], _t5_lookup(t5_tab_a, rel), NEG_INF)
    return bias.reshape(N_KV_A, GQA_GROUP * BLOCK, 3 * BLOCK)


def _table_c(t5_tab_c):
    qq = jnp.arange(QT)[:, None]
    kk = jnp.arange(KT)[None, :]
    rel = jnp.stack([QT * (v - 3) + kk - qq for v in range(1, 5)])
    near = _t5_lookup(t5_tab_c, rel)
    far = _t5_lookup(t5_tab_c, jnp.array([-(KT + QT), KT + QT]))
    far = jnp.broadcast_to(far[:, :, None, None], (N_HEADS_C, 2, QT, KT))
    return jnp.concatenate([far[:, 0:1], near, far[:, 1:2]], axis=1)


def _table_b_static():
    rows_q, rows_k = QT // GRID_W, KWIN // GRID_W
    qc = np.arange(GRID_W)[:, None]
    kc = np.arange(GRID_W)[None, :]
    col_start = np.clip(qc - NA_COLS // 2, 0, GRID_W - NA_COLS)
    col_valid = (kc >= col_start) & (kc < col_start + NA_COLS)
    dc = np.clip(kc - qc, -(NA_COLS - 1), NA_COLS - 1) + NA_COLS - 1
    oh_c = (dc[None] == np.arange(2 * NA_COLS - 1)[:, None, None]).astype(np.float32)
    ty = np.arange(3)[:, None, None]
    ql = np.arange(rows_q)[None, :, None]
    kl = np.arange(rows_k)[None, None, :]
    dr = kl - rows_q * ty - ql + NA_ROWS - 1
    win_lo = np.where(ty == 0, 0, np.where(ty == 1, ql, rows_k - NA_ROWS))
    win = (kl >= win_lo) & (kl < win_lo + NA_ROWS)
    oh_r = ((dr[None] == np.arange(2 * NA_ROWS - 1)[:, None, None, None]) & win[None]).astype(np.float32)
    valid = win[:, :, None, :, None] & col_valid[None, None, :, None, :]
    return oh_c, oh_r, valid.reshape(3, QT, KWIN)


_TB_OH_C, _TB_OH_R, _TB_VALID = _table_b_static()


def _table_b(rpb):
    hp = lax.Precision.HIGHEST
    cols = jnp.einsum('hrc,cqk->hrqk', rpb.astype(F32), _TB_OH_C, precision=hp)
    bias = jnp.einsum('hrqk,rtal->htaqlk', cols, _TB_OH_R, precision=hp).reshape(N_HEADS_B, 3, QT, KWIN)
    return jnp.where(_TB_VALID[None], bias, NEG_INF)


def _attn_a_kernel(q_ref, kp_ref, kc_ref, kn_ref, vp_ref, vc_ref, vn_ref, tab_ref, sink_ref, o_ref, *, npair):
    i = pl.program_id(2)
    k4 = jnp.concatenate([kp_ref[...], kc_ref[...], kn_ref[...]], axis=0)
    v4 = jnp.concatenate([vp_ref[...], vc_ref[...], vn_ref[...]], axis=0)
    col = lax.broadcasted_iota(I32, (1, 3 * BLOCK), 1)
    sk = sink_ref[0]
    for u in range(2):
        q = q_ref[u * BLOCK:(u + 1) * BLOCK, :]
        q4 = jnp.concatenate([q[:, c * HEAD_DIM:(c + 1) * HEAD_DIM] for c in range(GQA_GROUP)], axis=0)
        k3 = k4[u * BLOCK:(u + 3) * BLOCK]
        v3 = v4[u * BLOCK:(u + 3) * BLOCK]
        s = lax.dot_general(q4, k3, (((1,), (1,)), ((), ())), preferred_element_type=F32)
        s = s * (HEAD_DIM ** -0.5) + tab_ref[0]
        if u == 0:
            lo, hi = jnp.where(i == 0, BLOCK, 0), 3 * BLOCK
        else:
            lo, hi = 0, jnp.where(i == npair - 1, 2 * BLOCK, 3 * BLOCK)
        s = jnp.where((col < lo) | (col >= hi), NEG_INF, s)
        m = jnp.maximum(jnp.max(s, axis=-1, keepdims=True), sk)
        p = jnp.exp(s - m)
        l = jnp.sum(p, axis=-1, keepdims=True) + jnp.exp(sk - m)
        o = jnp.dot(p.astype(BF16), v3, preferred_element_type=F32) / l
        o_ref[u * BLOCK:(u + 1) * BLOCK, :] = jnp.concatenate(
            [o[c * BLOCK:(c + 1) * BLOCK] for c in range(GQA_GROUP)], axis=1).astype(o_ref.dtype)


def _attn_a(h, tab_a, sink_col, bsz, t):
    nb = t // BLOCK
    npair = nb // 2
    n = bsz * t

    def edge_spec(cb, blk):
        return pl.BlockSpec((BLOCK, HEAD_DIM), lambda b, g, i: (b * nb + jnp.clip(blk(i), 0, nb - 1), cb + g))

    def pair_spec(cb):
        return pl.BlockSpec((2 * BLOCK, HEAD_DIM), lambda b, g, i: (b * npair + i, cb + g))

    return pl.pallas_call(
        functools.partial(_attn_a_kernel, npair=npair),
        grid=(bsz, N_KV_A, npair),
        in_specs=[pl.BlockSpec((2 * BLOCK, GQA_GROUP * HEAD_DIM), lambda b, g, i: (b * npair + i, g)),
                  edge_spec(KA_CB, lambda i: 2 * i - 1), pair_spec(KA_CB), edge_spec(KA_CB, lambda i: 2 * i + 2),
                  edge_spec(VA_CB, lambda i: 2 * i - 1), pair_spec(VA_CB), edge_spec(VA_CB, lambda i: 2 * i + 2),
                  pl.BlockSpec((1, GQA_GROUP * BLOCK, 3 * BLOCK), lambda b, g, i: (g, 0, 0)),
                  pl.BlockSpec((1, GQA_GROUP * BLOCK, 1), lambda b, g, i: (g, 0, 0))],
        out_specs=pl.BlockSpec((2 * BLOCK, GQA_GROUP * HEAD_DIM), lambda b, g, i: (b * npair + i, g)),
        out_shape=jax.ShapeDtypeStruct((n, N_HEADS_A * HEAD_DIM), BF16),
        compiler_params=_cparams(("parallel", "parallel", "arbitrary")),
        name="attn_window",
    )(h, h, h, h, h, h, h, tab_a, sink_col)


def _attn_b_kernel(q_ref, k0_ref, k1_ref, k2_ref, v0_ref, v1_ref, v2_ref, tab_ref, o_ref):
    q = q_ref[...]
    k3 = jnp.concatenate([k0_ref[...], k1_ref[...], k2_ref[...]], axis=0)
    v3 = jnp.concatenate([v0_ref[...], v1_ref[...], v2_ref[...]], axis=0)
    s = lax.dot_general(q, k3, (((1,), (1,)), ((), ())), preferred_element_type=F32)
    s = s * (HEAD_DIM ** -0.5) + tab_ref[0, 0]
    m = jnp.max(s, axis=-1, keepdims=True)
    p = jnp.exp(s - m)
    l = jnp.sum(p, axis=-1, keepdims=True)
    o = jnp.dot(p.astype(BF16), v3, preferred_element_type=F32) / l
    o_ref[...] = o.astype(o_ref.dtype)


def _attn_b(h, tab_b, bsz, t):
    nq = t // QT
    n = bsz * t

    def kv_spec(cb, j):
        return pl.BlockSpec((QT, HEAD_DIM), lambda b, hd, g: (b * nq + jnp.clip(g - 1, 0, nq - 3) + j, cb + hd))

    def tab_map(b, hd, g):
        ty = jnp.where(g == 0, 0, jnp.where(g == nq - 1, 2, 1))
        return (hd, ty, 0, 0)

    return pl.pallas_call(
        _attn_b_kernel,
        grid=(bsz, N_HEADS_B, nq),
        in_specs=[pl.BlockSpec((QT, HEAD_DIM), lambda b, hd, g: (b * nq + g, QB_CB + hd)),
                  kv_spec(KB_CB, 0), kv_spec(KB_CB, 1), kv_spec(KB_CB, 2),
                  kv_spec(VB_CB, 0), kv_spec(VB_CB, 1), kv_spec(VB_CB, 2),
                  pl.BlockSpec((1, 1, QT, KWIN), tab_map)],
        out_specs=pl.BlockSpec((QT, HEAD_DIM), lambda b, hd, g: (b * nq + g, hd)),
        out_shape=jax.ShapeDtypeStruct((n, N_HEADS_B * HEAD_DIM), BF16),
        compiler_params=_cparams(("parallel", "parallel", "arbitrary")),
        name="attn_nbr",
    )(h, h, h, h, h, h, h, tab_b)


def _attn_c_kernel(lam_ref, q_ref, k_ref, v_ref, tab_ref, g_ref, o_ref,
                   vaug_ref, m_ref, acc_ref, *, t, lam_init):
    i = pl.program_id(2)

    @pl.when(i == 0)
    def _():
        lane = lax.broadcasted_iota(I32, (t, HEAD_DIM), 1)
        vaug_ref[:, :HEAD_DIM] = v_ref[...]
        vaug_ref[:, HEAD_DIM:] = jnp.where(lane == 0, 1.0, 0.0).astype(BF16)

    q = q_ref[...] * jnp.asarray(DIFF_QK_DIM ** -0.5, BF16)
    qs = (q[:, :DIFF_QK_DIM], q[:, DIFF_QK_DIM:])
    m_ref[...] = jnp.full(m_ref.shape, NEG_INF, F32)
    acc_ref[...] = jnp.zeros(acc_ref.shape, F32)

    for j in range(t // KT):
        keys = slice(j * KT, (j + 1) * KT)
        bias = tab_ref[0, jnp.clip(2 * j - i + 3, 0, 5)]
        kt = k_ref[keys, :]
        for c in range(2):
            kc = kt[:, c * DIFF_QK_DIM:(c + 1) * DIFF_QK_DIM]
            s = lax.dot_general(qs[c], kc, (((1,), (1,)), ((), ())), preferred_element_type=F32) + bias
            m_old = m_ref[c]
            m_new = jnp.maximum(m_old, jnp.max(s, axis=-1, keepdims=True))
            a = jnp.exp(m_old - m_new)
            p = jnp.exp(s - jnp.concatenate([m_new] * (KT // LANES), axis=1))
            pv = jnp.dot(p.astype(BF16), vaug_ref[keys, :], preferred_element_type=F32)
            acc_ref[c] = jnp.concatenate([a, a], axis=1) * acc_ref[c] + pv
            m_ref[c] = m_new

    def normalised(c):
        acc = acc_ref[c]
        return acc[:, :HEAD_DIM] / acc[:, HEAD_DIM:HEAD_DIM + 1]

    o = normalised(0) - lam_ref[0] * normalised(1)
    o = o * lax.rsqrt(jnp.mean(jnp.square(o), axis=-1, keepdims=True) + LN_EPS) * g_ref[...] * (1.0 - lam_init)
    o_ref[...] = o.astype(o_ref.dtype)


def _attn_c(h, tab_c, lam, subln_g, bsz, t, lam_init):
    nq = t // QT
    n = bsz * t
    grid_spec = pltpu.PrefetchScalarGridSpec(
        num_scalar_prefetch=1,
        grid=(bsz, N_HEADS_C, nq),
        in_specs=[pl.BlockSpec((QT, HEAD_DIM), lambda b, hd, i, s: (b * nq + i, QC_CB + hd)),
                  pl.BlockSpec((t, HEAD_DIM), lambda b, hd, i, s: (b, KC_CB + hd)),
                  pl.BlockSpec((t, HEAD_DIM), lambda b, hd, i, s: (b, VC_CB + hd)),
                  pl.BlockSpec((1, 6, QT, KT), lambda b, hd, i, s: (hd, 0, 0, 0)),
                  pl.BlockSpec((1, HEAD_DIM), lambda b, hd, i, s: (0, 0))],
        out_specs=pl.BlockSpec((QT, HEAD_DIM), lambda b, hd, i, s: (b * nq + i, hd)),
        scratch_shapes=[pltpu.VMEM((t, 2 * HEAD_DIM), BF16),
                        pltpu.VMEM((2, QT, LANES), F32),
                        pltpu.VMEM((2, QT, 2 * HEAD_DIM), F32)],
    )
    return pl.pallas_call(
        functools.partial(_attn_c_kernel, t=t, lam_init=lam_init),
        grid_spec=grid_spec,
        out_shape=jax.ShapeDtypeStruct((n, N_HEADS_C * HEAD_DIM), BF16),
        compiler_params=_cparams(("parallel", "parallel", "arbitrary")),
        name="attn_diff",
    )(lam, h, h, h, tab_c, subln_g)


def _split_bf16(x):
    hi = x.astype(BF16)
    lo = (x - hi.astype(F32)).astype(BF16)
    return hi, lo


def _out_ln_router_kernel(oa_ref, ob_ref, oc_ref, w_ref, x_ref, g_ref, b_ref, wr_ref,
                          xe_ref, afft_ref, pre_ref, *, nj, tn):
    j = pl.program_id(1)
    ca, cb = oa_ref.shape[1], ob_ref.shape[1]
    acc = jnp.dot(oa_ref[...], w_ref[0:ca, :], preferred_element_type=F32)
    acc += jnp.dot(ob_ref[...], w_ref[ca:ca + cb, :], preferred_element_type=F32)
    acc += jnp.dot(oc_ref[...], w_ref[ca + cb:, :], preferred_element_type=F32)
    pre_ref[j] = ALPHA * x_ref[...] + acc

    @pl.when(j == nj - 1)
    def _():
        tm = pre_ref.shape[1]
        d = nj * tn
        tot = jnp.zeros((tm, 1), F32)
        for c in range(nj):
            tot += jnp.sum(pre_ref[c], axis=-1, keepdims=True)
        mu = tot / d
        sq = jnp.zeros((tm, 1), F32)
        for c in range(nj):
            sq += jnp.sum(jnp.square(pre_ref[c] - mu), axis=-1, keepdims=True)
        rstd = lax.rsqrt(sq / d + LN_EPS)
        logits = jnp.zeros((tm, LANES), F32)
        for c in range(nj):
            sl = slice(c * tn, (c + 1) * tn)
            y = (pre_ref[c] - mu) * rstd * g_ref[:, sl] + b_ref[:, sl]
            xe_ref[:, sl] = y
            yh, yl = _split_bf16(y)
            wh, wl = _split_bf16(wr_ref[sl, :])
            logits += jnp.dot(yh, wh, preferred_element_type=F32)
            logits += jnp.dot(yh, wl, preferred_element_type=F32)
            logits += jnp.dot(yl, wh, preferred_element_type=F32)
        lane = lax.broadcasted_iota(I32, (1, LANES), 1)
        logits = jnp.where(lane < N_EXPERTS, logits, NEG_INF)
        mx = jnp.max(logits, axis=-1, keepdims=True)
        ex = jnp.exp(logits - mx)
        aff = ex / jnp.sum(ex, axis=-1, keepdims=True)
        xe_ref[:, d:] = aff
        afft_ref[...] = aff.T[:N_EXPERTS, :]


def _out_ln_router(oa, ob, oc, w_out, l, x, g, b, w_router_pad):
    n, d = x.shape
    tm, tn = 512, 512
    nj = d // tn
    return pl.pallas_call(
        functools.partial(_out_ln_router_kernel, nj=nj, tn=tn),
        grid=(n // tm, nj),
        in_specs=[pl.BlockSpec((tm, oa.shape[1]), lambda i, j: (i, 0)),
                  pl.BlockSpec((tm, ob.shape[1]), lambda i, j: (i, 0)),
                  pl.BlockSpec((tm, oc.shape[1]), lambda i, j: (i, 0)),
                  pl.BlockSpec((None, d, tn), lambda i, j: (l, 0, j)),
                  pl.BlockSpec((tm, tn), lambda i, j: (i, j)),
                  pl.BlockSpec((1, d), lambda i, j: (0, 0)),
                  pl.BlockSpec((1, d), lambda i, j: (0, 0)),
                  pl.BlockSpec((d, LANES), lambda i, j: (0, 0))],
        out_specs=[pl.BlockSpec((tm, d + LANES), lambda i, j: (i, 0)),
                   pl.BlockSpec((N_EXPERTS, tm), lambda i, j: (0, i))],
        out_shape=[jax.ShapeDtypeStruct((n, d + LANES), F32),
                   jax.ShapeDtypeStruct((N_EXPERTS, n), F32)],
        scratch_shapes=[pltpu.VMEM((nj, tm, tn), F32)],
        compiler_params=_cparams(("parallel", "arbitrary")),
        name="out_proj_ln_router",
    )(oa, ob, oc, w_out, x, g, b, w_router_pad)


def _route_kernel(aff_ref, idx_ref, incl_ref, incl8_ref, *, cap):
    e_cnt, n = aff_ref.shape
    nch = n // LANES
    bits = pltpu.bitcast(aff_ref[...], I32)

    def count(mask):
        return jnp.sum(jnp.where(mask, 1.0, 0.0), axis=1, keepdims=True)

    def search(it, v):
        cand = v | jnp.left_shift(jnp.int32(1), 30 - it)
        return jnp.where(count(bits >= cand) >= cap, cand, v)

    thr = lax.fori_loop(0, 31, search, jnp.zeros((e_cnt, 1), I32))
    gtf = jnp.where(bits > thr, 1.0, 0.0)
    eqf = jnp.where(bits == thr, 1.0, 0.0)
    need = cap - jnp.sum(gtf, axis=1, keepdims=True)

    r = lax.broadcasted_iota(I32, (LANES, LANES), 0)
    c = lax.broadcasted_iota(I32, (LANES, LANES), 1)
    tri = jnp.where(r <= c, 1.0, 0.0).astype(BF16)
    ones = jnp.ones((LANES, LANES), BF16)

    def prefix(maskf):
        carry = jnp.zeros((e_cnt, LANES), F32)
        for j in range(nch):
            sl = slice(j * LANES, (j + 1) * LANES)
            mj = maskf[:, sl].astype(BF16)
            incl_ref[:, sl] = jnp.dot(mj, tri, preferred_element_type=F32) + carry
            carry = carry + jnp.dot(mj, ones, preferred_element_type=F32)

    prefix(eqf)
    tie_rank = incl_ref[...] - eqf
    chosen = gtf + eqf * jnp.where(tie_rank < need, 1.0, 0.0)
    prefix(chosen)
    for e in range(e_cnt):
        incl8_ref[e] = jnp.broadcast_to(incl_ref[e:e + 1, :], (8, n))

    ones_row = jnp.ones((8, LANES), BF16)
    slot_iota = lax.broadcasted_iota(I32, (LANES, 1), 0)

    def compact(it, carry):
        e = it // (cap // LANES)
        sb = it % (cap // LANES)
        slot = (slot_iota + sb * LANES).astype(F32)
        acc = jnp.zeros((LANES, LANES), F32)
        for j in range(nch):
            blk = incl8_ref[e, :, j * LANES:(j + 1) * LANES]
            acc += jnp.where(jnp.concatenate([blk] * (LANES // 8), axis=0) <= slot, 1.0, 0.0)
        tot = lax.dot_general(ones_row, acc.astype(BF16), (((1,), (1,)), ((), ())), preferred_element_type=F32)
        idx_ref[e, sb] = tot.astype(I32)
        return carry

    lax.fori_loop(0, e_cnt * (cap // LANES), compact, 0)


def _route(aff_t, cap):
    e_cnt, n = aff_t.shape
    idx = pl.pallas_call(
        functools.partial(_route_kernel, cap=cap),
        out_shape=jax.ShapeDtypeStruct((e_cnt, cap // LANES, 8, LANES), I32),
        scratch_shapes=[pltpu.VMEM((e_cnt, n), F32), pltpu.VMEM((e_cnt, 8, n), F32)],
        compiler_params=pltpu.CompilerParams(vmem_limit_bytes=VMEM_LIMIT),
        name="route",
    )(aff_t)
    return idx[:, :, 0, :].reshape(e_cnt, 1, cap)


def _ffn_kernel(idx_ref, idxn_ref, x_hbm, wg_ref, wu_ref, wd_ref, y_ref,
                xe_ref, h_ref, gate_ref, stage_ref, sem, *, cap, nf, tf, d, rps):
    e = pl.program_id(0)
    s = pl.program_id(1)
    ne = pl.num_programs(0)
    ns = pl.num_programs(1)
    nchunk = cap // rps
    cur = e % 2
    e_next = jnp.minimum(e + 1, ne - 1)
    lane = lax.broadcasted_iota(I32, (1, LANES), 1)

    def row_copy(row, slot, r):
        return pltpu.make_async_copy(x_hbm.at[pl.ds(row, 1), :], stage_ref.at[slot, pl.ds(r, 1), :], sem.at[slot])

    def land(chunk, slot, buf, expert):
        pltpu.make_async_copy(x_hbm.at[pl.ds(0, rps), :], stage_ref.at[slot], sem.at[slot]).wait()
        rows = pl.ds(pl.multiple_of(chunk * rps, rps), rps)
        xe_ref[buf, rows, :] = stage_ref[slot, :, 0:d].astype(BF16)
        aff = stage_ref[slot, :, d:d + LANES]
        gate_ref[buf, rows, :] = jnp.sum(jnp.where(lane == expert, aff, 0.0), axis=-1, keepdims=True)

    @pl.when((e == 0) & (s == 0))
    def _():
        def fetch(c, carry):
            slot = c % 2

            def start(r, carry2):
                row_copy(idx_ref[0, 0, c * rps + r], slot, r).start()
                return carry2
            lax.fori_loop(0, rps, start, 0)

            @pl.when(c < nchunk - 1)
            def _():
                land(c, slot, 0, 0)
            return carry
        lax.fori_loop(0, nchunk, fetch, 0)

    def step_gather():
        first = s == 0
        land(jnp.where(first, nchunk - 1, s - 1), (s + 1) % 2,
             jnp.where(first, cur, 1 - cur), jnp.where(first, e, e_next))
        slot = s % 2
        for r in range(rps):
            row_copy(idxn_ref[0, 0, s * rps + r], slot, r).start()

    @pl.when(s < nf)
    def _():
        step_gather()
        xe = xe_ref[cur]
        g = jnp.dot(xe, wg_ref[...].astype(BF16), preferred_element_type=F32)
        u = jnp.dot(xe, wu_ref[...].astype(BF16), preferred_element_type=F32)
        hv = (g * (1.0 / (1.0 + jnp.exp(-g))) * u).astype(BF16)
        for c in range(nf):
            @pl.when(s == c)
            def _():
                h_ref[:, c * tf:(c + 1) * tf] = hv

    @pl.when(s >= nf)
    def _():
        step_gather()
        y = jnp.dot(h_ref[...], wd_ref[...].astype(BF16), preferred_element_type=F32)
        y_ref[...] = y * gate_ref[cur]

    @pl.when((e == ne - 1) & (s == ns - 1))
    def _():
        pltpu.make_async_copy(x_hbm.at[pl.ds(0, rps), :], stage_ref.at[(nchunk - 1) % 2],
                              sem.at[(nchunk - 1) % 2]).wait()


def _expert_ffn(idx, x_ext, w_gate, w_up, w_down, l, cap):
    _, e_cnt, d, f = w_gate.shape
    tf, tn = 256, 512
    nf, nn = f // tf, d // tn
    rps = cap // (nf + nn)
    assert rps * (nf + nn) == cap and rps % 16 == 0 and (nf + nn) % 2 == 0
    return pl.pallas_call(
        functools.partial(_ffn_kernel, cap=cap, nf=nf, tf=tf, d=d, rps=rps),
        grid=(e_cnt, nf + nn),
        in_specs=[pl.BlockSpec((1, 1, cap), lambda e, s: (e, 0, 0), memory_space=pltpu.SMEM),
                  pl.BlockSpec((1, 1, cap), lambda e, s: (jnp.minimum(e + 1, e_cnt - 1), 0, 0),
                               memory_space=pltpu.SMEM),
                  pl.BlockSpec(memory_space=pl.ANY),
                  pl.BlockSpec((None, None, d, tf), lambda e, s: (l, e, 0, jnp.minimum(s, nf - 1))),
                  pl.BlockSpec((None, None, d, tf), lambda e, s: (l, e, 0, jnp.minimum(s, nf - 1))),
                  pl.BlockSpec((None, None, f, tn), lambda e, s: (l, e, 0, jnp.maximum(s - nf, 0)))],
        out_specs=pl.BlockSpec((cap, tn), lambda e, s: (e, jnp.maximum(s - nf, 0))),
        out_shape=jax.ShapeDtypeStruct((e_cnt * cap, d), F32),
        scratch_shapes=[pltpu.VMEM((2, cap, d), BF16),
                        pltpu.VMEM((cap, f), BF16),
                        pltpu.VMEM((2, cap, 1), F32),
                        pltpu.VMEM((2, rps, x_ext.shape[1]), F32),
                        pltpu.SemaphoreType.DMA((2,))],
        compiler_params=_cparams(("arbitrary", "arbitrary")),
        name="expert_ffn",
    )(idx, idx, x_ext, w_gate, w_up, w_down)


def _combine_kernel(idx_ref, y_hbm, acc_in_hbm, acc_hbm, buf_ref, ybuf_ref, gsem, ssem, ysem, *, cap):
    del acc_in_hbm
    e = pl.program_id(0)
    nchunk = cap // GATHER_ROWS

    def y_copy(c, slot):
        return pltpu.make_async_copy(y_hbm.at[pl.ds(e * cap + c * GATHER_ROWS, GATHER_ROWS), :],
                                     ybuf_ref.at[slot], ysem.at[slot])

    def row_in(row, slot, r):
        return pltpu.make_async_copy(acc_hbm.at[pl.ds(row, 1), :], buf_ref.at[slot, pl.ds(r, 1), :], gsem.at[slot])

    def row_out(row, slot, r):
        return pltpu.make_async_copy(buf_ref.at[slot, pl.ds(r, 1), :], acc_hbm.at[pl.ds(row, 1), :], ssem.at[slot])

    def for_rows(fn):
        def body(r, carry):
            fn(r)
            return carry
        lax.fori_loop(0, GATHER_ROWS, body, 0, unroll=8)

    def start_gather(c, slot):
        y_copy(c, slot).start()
        for_rows(lambda r: row_in(idx_ref[0, 0, c * GATHER_ROWS + r], slot, r).start())

    def wait_gather(c, slot):
        y_copy(c, slot).wait()
        pltpu.make_async_copy(acc_hbm.at[pl.ds(0, GATHER_ROWS), :], buf_ref.at[slot], gsem.at[slot]).wait()

    def start_scatter(c, slot):
        for_rows(lambda r: row_out(idx_ref[0, 0, c * GATHER_ROWS + r], slot, r).start())

    def wait_scatter(slot):
        pltpu.make_async_copy(buf_ref.at[slot], acc_hbm.at[pl.ds(0, GATHER_ROWS), :], ssem.at[slot]).wait()

    start_gather(0, 0)
    for c in range(nchunk):
        slot = c % 2
        if c + 1 < nchunk:
            if c >= 1:
                wait_scatter(1 - slot)
            start_gather(c + 1, 1 - slot)
        wait_gather(c, slot)
        buf_ref[slot] = buf_ref[slot] + ybuf_ref[slot]
        start_scatter(c, slot)
    if nchunk >= 2:
        wait_scatter(nchunk % 2)
    wait_scatter((nchunk - 1) % 2)


def _combine(idx, y, acc, cap):
    e_cnt = idx.shape[0]
    n, d = acc.shape
    return pl.pallas_call(
        functools.partial(_combine_kernel, cap=cap),
        grid=(e_cnt,),
        in_specs=[pl.BlockSpec((1, 1, cap), lambda e: (e, 0, 0), memory_space=pltpu.SMEM),
                  pl.BlockSpec(memory_space=pl.ANY),
                  pl.BlockSpec(memory_space=pl.ANY)],
        out_specs=pl.BlockSpec(memory_space=pl.ANY),
        out_shape=jax.ShapeDtypeStruct((n, d), F32),
        input_output_aliases={2: 0},
        scratch_shapes=[pltpu.VMEM((2, GATHER_ROWS, d), F32),
                        pltpu.VMEM((2, GATHER_ROWS, d), F32),
                        pltpu.SemaphoreType.DMA((2,)),
                        pltpu.SemaphoreType.DMA((2,)),
                        pltpu.SemaphoreType.DMA((2,))],
        compiler_params=_cparams(("arbitrary",)),
        name="combine",
    )(idx, y, acc)


def _ln2_kernel(x_ref, s_ref, g_ref, b_ref, o_ref, ob_ref):
    pre = ALPHA * x_ref[...] + s_ref[...]
    mu = jnp.mean(pre, axis=-1, keepdims=True)
    var = jnp.mean(jnp.square(pre - mu), axis=-1, keepdims=True)
    y = (pre - mu) * lax.rsqrt(var + LN_EPS) * g_ref[...] + b_ref[...]
    o_ref[...] = y
    ob_ref[...] = y.astype(BF16)


def _ln2(x_ext, moe, g, b):
    n, d = moe.shape
    tm = 256
    return pl.pallas_call(
        _ln2_kernel,
        grid=(n // tm,),
        in_specs=[pl.BlockSpec((tm, d), lambda i: (i, 0)),
                  pl.BlockSpec((tm, d), lambda i: (i, 0)),
                  pl.BlockSpec((1, d), lambda i: (0, 0)),
                  pl.BlockSpec((1, d), lambda i: (0, 0))],
        out_specs=[pl.BlockSpec((tm, d), lambda i: (i, 0)),
                   pl.BlockSpec((tm, d), lambda i: (i, 0))],
        out_shape=[jax.ShapeDtypeStruct((n, d), F32), jax.ShapeDtypeStruct((n, d), BF16)],
        compiler_params=_cparams(("parallel",)),
        name="ln2",
    )(x_ext, moe, g, b)


def _layer(x, xb, bsz, t, l, lam_init, shared, p):
    n = bsz * t
    cap = EC_CAPACITY_FACTOR * n // N_EXPERTS
    h = _in_proj(xb, shared["w_in"], l)
    oa = _attn_a(h, shared["tab_a"], p["sink_col"], bsz, t)
    ob = _attn_b(h, p["tab_b"], bsz, t)
    oc = _attn_c(h, shared["tab_c"], p["lam"], p["subln"], bsz, t, lam_init)
    x_ext, aff_t = _out_ln_router(oa, ob, oc, shared["w_out"], l, x, p["ln1_g"], p["ln1_b"], p["w_router"])
    idx = _route(aff_t, cap)
    y = _expert_ffn(idx, x_ext, shared["w_gate"], shared["w_up"], shared["w_down"], l, cap)
    moe = _combine(idx, y, jnp.zeros((n, D_MODEL), F32), cap)
    return _ln2(x_ext, moe, p["ln2_g"], p["ln2_b"])


def _trunk(x3, shared, layers):
    bsz, t, d = x3.shape
    x = x3.reshape(bsz * t, d)
    xb = x.astype(BF16)
    for l in range(DEPTH):
        lam_init = 0.8 - 0.6 * math.exp(-0.3 * l)
        x, xb = _layer(x, xb, bsz, t, l, lam_init, shared, layers[l])
    return x.reshape(bsz, t, d)


def kernel(x_prompt, x_sample, w_in, w_out, ln1_g, ln1_b, ln2_g, ln2_b, sink_a, rpb_b, diff_lambda, subln_c,
           t5_table, w_router, w_gate, w_up, w_down):
    shared = dict(
        tab_a=_table_a(t5_table[:, :N_HEADS_A]), tab_c=_table_c(t5_table[:, N_HEADS_A:]),
        w_in=w_in.astype(BF16), w_out=w_out.astype(BF16), w_gate=w_gate, w_up=w_up, w_down=w_down)
    layers = []
    for l in range(DEPTH):
        lam_init = 0.8 - 0.6 * math.exp(-0.3 * l)
        lf = diff_lambda[l].astype(F32)
        lam = jnp.exp(jnp.sum(lf[0] * lf[1])) - jnp.exp(jnp.sum(lf[2] * lf[3])) + lam_init
        sink_col = jnp.repeat(sink_a[l].astype(F32), BLOCK).reshape(N_KV_A, GQA_GROUP * BLOCK, 1)
        layers.append(dict(
            ln1_g=ln1_g[l][None], ln1_b=ln1_b[l][None], ln2_g=ln2_g[l][None], ln2_b=ln2_b[l][None],
            sink_col=sink_col, tab_b=_table_b(rpb_b[l]), lam=lam[None].astype(F32),
            subln=subln_c[l][None].astype(F32),
            w_router=jnp.pad(w_router[l], ((0, 0), (0, LANES - N_EXPERTS)))))
    return (_trunk(x_prompt, shared, layers), _trunk(x_sample, shared, layers))
```

```python
import functools
import math

import numpy as np
import jax
import jax.numpy as jnp
from jax import lax
from jax.experimental import pallas as pl
from jax.experimental.pallas import tpu as pltpu

F32 = jnp.float32
BF16 = jnp.bfloat16
I32 = jnp.int32

D_MODEL = 4096
DEPTH = 4
HEAD_DIM = 128
N_HEADS_A = 16
N_KV_A = 4
GQA_GROUP = 4
WINDOW = 128
BLOCK = 128
N_HEADS_B = 8
GRID_W = 64
NA_ROWS = 8
NA_COLS = 16
N_HEADS_C = 8
DIFF_QK_DIM = 64
T5_BUCKETS = 32
T5_MAX_DIST = 128
N_EXPERTS = 16
EC_CAPACITY_FACTOR = 2
D_EXPERT = 2048
LN_EPS = 1e-5
NEG_INF = -1e30
ALPHA = (2 * DEPTH) ** 0.25
IN_COLS = 9216

QA_CB, KA_CB, VA_CB = 0, 16, 20
QB_CB, KB_CB, VB_CB = 24, 32, 40
QC_CB, KC_CB, VC_CB = 48, 56, 64

LANES = 128
EXT_COLS = D_MODEL + LANES
VMEM_LIMIT = 56 * 1024 * 1024

QT = 256
KWIN = 768
KT = 512
GATHER_ROWS = 256


def _cparams(sem):
    return pltpu.CompilerParams(dimension_semantics=sem, vmem_limit_bytes=VMEM_LIMIT)


W_STREAMS = 4


def _mm_kernel(x_ref, *refs):
    w_refs, o_ref = refs[:-1], refs[-1]
    x = x_ref[...]
    tn = w_refs[0].shape[1]
    for c, w_ref in enumerate(w_refs):
        o_ref[:, c * tn:(c + 1) * tn] = jnp.dot(x, w_ref[...], preferred_element_type=F32).astype(o_ref.dtype)


def _in_proj(xb, w, l):
    n, d = xb.shape
    cols = w.shape[2]
    tm, tn = 1024, 1024
    ts = tn // W_STREAMS

    def w_spec(c):
        return pl.BlockSpec((None, d, ts), lambda i, j: (l, 0, j * W_STREAMS + c))

    return pl.pallas_call(
        _mm_kernel,
        grid=(n // tm, cols // tn),
        in_specs=[pl.BlockSpec((tm, d), lambda i, j: (i, 0))] + [w_spec(c) for c in range(W_STREAMS)],
        out_specs=pl.BlockSpec((tm, tn), lambda i, j: (i, j)),
        out_shape=jax.ShapeDtypeStruct((n, cols), BF16),
        compiler_params=_cparams(("parallel", "arbitrary")),
        name="in_proj",
    )(xb, *([w] * W_STREAMS))


def _t5_bucket(rel):
    half = T5_BUCKETS // 2
    max_exact = half // 2
    n = jnp.abs(rel)
    n_f = jnp.maximum(n, 1).astype(F32)
    large = max_exact + (jnp.log(n_f / max_exact) / math.log(T5_MAX_DIST / max_exact) * (half - max_exact)).astype(I32)
    large = jnp.minimum(large, half - 1)
    return jnp.where(rel > 0, half, 0) + jnp.where(n < max_exact, n, large)


def _t5_lookup(table, rel):
    onehot = (_t5_bucket(rel)[..., None] == jnp.arange(T5_BUCKETS)).astype(F32)
    return jnp.einsum('...b,bh->h...', onehot, table.astype(F32), precision=lax.Precision.HIGHEST)


def _table_a(t5_tab_a):
    rel = jnp.arange(3 * BLOCK)[None, :] - BLOCK - jnp.arange(BLOCK)[:, None]
    bias = jnp.where((jnp.abs(rel) <= WINDOW)[None], _t5_lookup(t5_tab_a, rel), NEG_INF)
    return bias.reshape(N_KV_A, GQA_GROUP * BLOCK, 3 * BLOCK)


def _table_c(t5_tab_c):
    qq = jnp.arange(QT)[:, None]
    kk = jnp.arange(KT)[None, :]
    rel = jnp.stack([QT * (v - 3) + kk - qq for v in range(1, 5)])
    near = _t5_lookup(t5_tab_c, rel)
    far = _t5_lookup(t5_tab_c, jnp.array([-(KT + QT), KT + QT]))
    far = jnp.broadcast_to(far[:, :, None, None], (N_HEADS_C, 2, QT, KT))
    return jnp.concatenate([far[:, 0:1], near, far[:, 1:2]], axis=1)


def _table_b_static():
    rows_q, rows_k = QT // GRID_W, KWIN // GRID_W
    qc = np.arange(GRID_W)[:, None]
    kc = np.arange(GRID_W)[None, :]
    col_start = np.clip(qc - NA_COLS // 2, 0, GRID_W - NA_COLS)
    col_valid = (kc >= col_start) & (kc < col_start + NA_COLS)
    dc = np.clip(kc - qc, -(NA_COLS - 1), NA_COLS - 1) + NA_COLS - 1
    oh_c = (dc[None] == np.arange(2 * NA_COLS - 1)[:, None, None]).astype(np.float32)
    ty = np.arange(3)[:, None, None]
    ql = np.arange(rows_q)[None, :, None]
    kl = np.arange(rows_k)[None, None, :]
    dr = kl - rows_q * ty - ql + NA_ROWS - 1
    win_lo = np.where(ty == 0, 0, np.where(ty == 1, ql, rows_k - NA_ROWS))
    win = (kl >= win_lo) & (kl < win_lo + NA_ROWS)
    oh_r = ((dr[None] == np.arange(2 * NA_ROWS - 1)[:, None, None, None]) & win[None]).astype(np.float32)
    valid = win[:, :, None, :, None] & col_valid[None, None, :, None, :]
    return oh_c, oh_r, valid.reshape(3, QT, KWIN)


_TB_OH_C, _TB_OH_R, _TB_VALID = _table_b_static()


def _table_b(rpb):
    hp = lax.Precision.HIGHEST
    cols = jnp.einsum('hrc,cqk->hrqk', rpb.astype(F32), _TB_OH_C, precision=hp)
    bias = jnp.einsum('hrqk,rtal->htaqlk', cols, _TB_OH_R, precision=hp).reshape(N_HEADS_B, 3, QT, KWIN)
    return jnp.where(_TB_VALID[None], bias, NEG_INF)


def _attn_a_kernel(q_ref, kp_ref, kc_ref, kn_ref, vp_ref, vc_ref, vn_ref, tab_ref, sink_ref, o_ref, *, npair):
    i = pl.program_id(2)
    k4 = jnp.concatenate([kp_ref[...], kc_ref[...], kn_ref[...]], axis=0)
    v4 = jnp.concatenate([vp_ref[...], vc_ref[...], vn_ref[...]], axis=0)
    col = lax.broadcasted_iota(I32, (1, 3 * BLOCK), 1)
    sk = sink_ref[0]
    ones = jnp.ones((3 * BLOCK, HEAD_DIM), BF16)
    for u in range(2):
        q = q_ref[u * BLOCK:(u + 1) * BLOCK, :]
        q4 = jnp.concatenate([q[:, c * HEAD_DIM:(c + 1) * HEAD_DIM] for c in range(GQA_GROUP)], axis=0)
        k3 = k4[u * BLOCK:(u + 3) * BLOCK]
        v3 = v4[u * BLOCK:(u + 3) * BLOCK]
        s = lax.dot_general(q4, k3, (((1,), (1,)), ((), ())), preferred_element_type=F32)
        s = s * (HEAD_DIM ** -0.5) + tab_ref[0]
        if u == 0:
            lo, hi = jnp.where(i == 0, BLOCK, 0), 3 * BLOCK
        else:
            lo, hi = 0, jnp.where(i == npair - 1, 2 * BLOCK, 3 * BLOCK)
        s = jnp.where((col < lo) | (col >= hi), NEG_INF, s)
        m = jnp.maximum(jnp.max(s, axis=-1, keepdims=True), sk)
        p = jnp.exp(s - jnp.concatenate([m] * 3, axis=1))
        pv = jnp.dot(p.astype(BF16), jnp.concatenate([v3, ones], axis=1), preferred_element_type=F32)
        o = pv[:, :HEAD_DIM] / (pv[:, HEAD_DIM:] + jnp.exp(sk - m))
        o_ref[u * BLOCK:(u + 1) * BLOCK, :] = jnp.concatenate(
            [o[c * BLOCK:(c + 1) * BLOCK] for c in range(GQA_GROUP)], axis=1).astype(o_ref.dtype)


def _attn_a(h, tab_a, sink_col, bsz, t):
    nb = t // BLOCK
    npair = nb // 2
    n = bsz * t

    def edge_spec(cb, blk):
        return pl.BlockSpec((BLOCK, HEAD_DIM), lambda b, g, i: (b * nb + jnp.clip(blk(i), 0, nb - 1), cb + g))

    def pair_spec(cb):
        return pl.BlockSpec((2 * BLOCK, HEAD_DIM), lambda b, g, i: (b * npair + i, cb + g))

    return pl.pallas_call(
        functools.partial(_attn_a_kernel, npair=npair),
        grid=(bsz, N_KV_A, npair),
        in_specs=[pl.BlockSpec((2 * BLOCK, GQA_GROUP * HEAD_DIM), lambda b, g, i: (b * npair + i, g)),
                  edge_spec(KA_CB, lambda i: 2 * i - 1), pair_spec(KA_CB), edge_spec(KA_CB, lambda i: 2 * i + 2),
                  edge_spec(VA_CB, lambda i: 2 * i - 1), pair_spec(VA_CB), edge_spec(VA_CB, lambda i: 2 * i + 2),
                  pl.BlockSpec((1, GQA_GROUP * BLOCK, 3 * BLOCK), lambda b, g, i: (g, 0, 0)),
                  pl.BlockSpec((1, GQA_GROUP * BLOCK, LANES), lambda b, g, i: (g, 0, 0))],
        out_specs=pl.BlockSpec((2 * BLOCK, GQA_GROUP * HEAD_DIM), lambda b, g, i: (b * npair + i, g)),
        out_shape=jax.ShapeDtypeStruct((n, N_HEADS_A * HEAD_DIM), BF16),
        compiler_params=_cparams(("parallel", "parallel", "arbitrary")),
        name="attn_window",
    )(h, h, h, h, h, h, h, tab_a, sink_col)


def _attn_b_kernel(q_ref, k0_ref, k1_ref, k2_ref, v0_ref, v1_ref, v2_ref, tab_ref, o_ref):
    k3 = jnp.concatenate([k0_ref[...], k1_ref[...], k2_ref[...]], axis=0)
    v3 = jnp.concatenate([v0_ref[...], v1_ref[...], v2_ref[...]], axis=0)
    ones = jnp.ones((KWIN, HEAD_DIM), BF16)
    for u in range(2):
        cols = slice(u * HEAD_DIM, (u + 1) * HEAD_DIM)
        s = lax.dot_general(q_ref[:, cols], k3[:, cols], (((1,), (1,)), ((), ())), preferred_element_type=F32)
        s = s * (HEAD_DIM ** -0.5) + tab_ref[u, 0]
        p = jnp.exp(s - jnp.max(s, axis=-1, keepdims=True))
        pv = jnp.dot(p.astype(BF16), jnp.concatenate([v3[:, cols], ones], axis=1), preferred_element_type=F32)
        o_ref[:, cols] = (pv[:, :HEAD_DIM] / pv[:, HEAD_DIM:]).astype(o_ref.dtype)


def _attn_b(h, tab_b, bsz, t):
    nq = t // QT
    n = bsz * t

    def kv_spec(cb, j):
        return pl.BlockSpec((QT, 2 * HEAD_DIM),
                            lambda b, hp, g: (b * nq + jnp.clip(g - 1, 0, nq - 3) + j, cb // 2 + hp))

    def tab_map(b, hp, g):
        ty = jnp.where(g == 0, 0, jnp.where(g == nq - 1, 2, 1))
        return (hp, ty, 0, 0)

    return pl.pallas_call(
        _attn_b_kernel,
        grid=(bsz, N_HEADS_B // 2, nq),
        in_specs=[pl.BlockSpec((QT, 2 * HEAD_DIM), lambda b, hp, g: (b * nq + g, QB_CB // 2 + hp)),
                  kv_spec(KB_CB, 0), kv_spec(KB_CB, 1), kv_spec(KB_CB, 2),
                  kv_spec(VB_CB, 0), kv_spec(VB_CB, 1), kv_spec(VB_CB, 2),
                  pl.BlockSpec((2, 1, QT, KWIN), tab_map)],
        out_specs=pl.BlockSpec((QT, 2 * HEAD_DIM), lambda b, hp, g: (b * nq + g, hp)),
        out_shape=jax.ShapeDtypeStruct((n, N_HEADS_B * HEAD_DIM), BF16),
        compiler_params=_cparams(("parallel", "parallel", "arbitrary")),
        name="attn_nbr",
    )(h, h, h, h, h, h, h, tab_b)


def _attn_c_kernel(lam_ref, q_ref, k_ref, v_ref, tab_ref, g_ref, o_ref,
                   vaug_ref, m_ref, acc_ref, *, t, lam_init):
    i = pl.program_id(2)

    @pl.when(i == 0)
    def _():
        vaug_ref[:, :HEAD_DIM] = v_ref[...]
        vaug_ref[:, HEAD_DIM:] = jnp.ones((t, HEAD_DIM), BF16)

    q = q_ref[...] * jnp.asarray(DIFF_QK_DIM ** -0.5, BF16)
    qs = (q[:, :DIFF_QK_DIM], q[:, DIFF_QK_DIM:])
    m_ref[...] = jnp.full(m_ref.shape, NEG_INF, F32)
    acc_ref[...] = jnp.zeros(acc_ref.shape, F32)

    for j in range(t // KT):
        keys = slice(j * KT, (j + 1) * KT)
        bias = tab_ref[0, jnp.clip(2 * j - i + 3, 0, 5)]
        kt = k_ref[keys, :]
        for c in range(2):
            kc = kt[:, c * DIFF_QK_DIM:(c + 1) * DIFF_QK_DIM]
            s = lax.dot_general(qs[c], kc, (((1,), (1,)), ((), ())), preferred_element_type=F32) + bias
            m_old = m_ref[c]
            m_new = jnp.maximum(m_old, jnp.max(s, axis=-1, keepdims=True))
            a = jnp.exp(m_old - m_new)
            p = jnp.exp(s - jnp.concatenate([m_new] * (KT // LANES), axis=1))
            pv = jnp.dot(p.astype(BF16), vaug_ref[keys, :], preferred_element_type=F32)
            acc_ref[c] = jnp.concatenate([a, a], axis=1) * acc_ref[c] + pv
            m_ref[c] = m_new

    def normalised(c):
        acc = acc_ref[c]
        return acc[:, :HEAD_DIM] / acc[:, HEAD_DIM:]

    o = normalised(0) - lam_ref[0] * normalised(1)
    o = o * lax.rsqrt(jnp.mean(jnp.square(o), axis=-1, keepdims=True) + LN_EPS) * g_ref[...] * (1.0 - lam_init)
    o_ref[...] = o.astype(o_ref.dtype)


def _attn_c(h, tab_c, lam, subln_g, bsz, t, lam_init):
    nq = t // QT
    n = bsz * t
    grid_spec = pltpu.PrefetchScalarGridSpec(
        num_scalar_prefetch=1,
        grid=(bsz, N_HEADS_C, nq),
        in_specs=[pl.BlockSpec((QT, HEAD_DIM), lambda b, hd, i, s: (b * nq + i, QC_CB + hd)),
                  pl.BlockSpec((t, HEAD_DIM), lambda b, hd, i, s: (b, KC_CB + hd)),
                  pl.BlockSpec((t, HEAD_DIM), lambda b, hd, i, s: (b, VC_CB + hd)),
                  pl.BlockSpec((1, 6, QT, KT), lambda b, hd, i, s: (hd, 0, 0, 0)),
                  pl.BlockSpec((1, HEAD_DIM), lambda b, hd, i, s: (0, 0))],
        out_specs=pl.BlockSpec((QT, HEAD_DIM), lambda b, hd, i, s: (b * nq + i, hd)),
        scratch_shapes=[pltpu.VMEM((t, 2 * HEAD_DIM), BF16),
                        pltpu.VMEM((2, QT, LANES), F32),
                        pltpu.VMEM((2, QT, 2 * HEAD_DIM), F32)],
    )
    return pl.pallas_call(
        functools.partial(_attn_c_kernel, t=t, lam_init=lam_init),
        grid_spec=grid_spec,
        out_shape=jax.ShapeDtypeStruct((n, N_HEADS_C * HEAD_DIM), BF16),
        compiler_params=_cparams(("parallel", "parallel", "arbitrary")),
        name="attn_diff",
    )(lam, h, h, h, tab_c, subln_g)


def _split_bf16(x):
    hi = x.astype(BF16)
    lo = (x - hi.astype(F32)).astype(BF16)
    return hi, lo


def _out_ln_router_kernel(oa_ref, ob_ref, oc_ref, w0_ref, w1_ref, w2_ref, w3_ref, x_ref, g_ref, b_ref, wr_ref,
                          xe_ref, afft_ref, pre_ref, *, nj, tn):
    j = pl.program_id(1)
    q = w0_ref.shape[0]
    acc = jnp.dot(oa_ref[:, 0:q], w0_ref[...], preferred_element_type=F32)
    acc += jnp.dot(oa_ref[:, q:2 * q], w1_ref[...], preferred_element_type=F32)
    acc += jnp.dot(ob_ref[...], w2_ref[...], preferred_element_type=F32)
    acc += jnp.dot(oc_ref[...], w3_ref[...], preferred_element_type=F32)
    pre_ref[j] = ALPHA * x_ref[...] + acc

    @pl.when(j == nj - 1)
    def _():
        tm = pre_ref.shape[1]
        d = nj * tn
        tot = jnp.zeros((tm, 1), F32)
        for c in range(nj):
            tot += jnp.sum(pre_ref[c], axis=-1, keepdims=True)
        mu = tot / d
        sq = jnp.zeros((tm, 1), F32)
        for c in range(nj):
            sq += jnp.sum(jnp.square(pre_ref[c] - mu), axis=-1, keepdims=True)
        rstd = lax.rsqrt(sq / d + LN_EPS)
        logits = jnp.zeros((tm, LANES), F32)
        for c in range(nj):
            sl = slice(c * tn, (c + 1) * tn)
            y = (pre_ref[c] - mu) * rstd * g_ref[:, sl] + b_ref[:, sl]
            xe_ref[:, sl] = y
            yh, yl = _split_bf16(y)
            wh, wl = _split_bf16(wr_ref[sl, :])
            logits += jnp.dot(yh, wh, preferred_element_type=F32)
            logits += jnp.dot(yh, wl, preferred_element_type=F32)
            logits += jnp.dot(yl, wh, preferred_element_type=F32)
        lane = lax.broadcasted_iota(I32, (1, LANES), 1)
        logits = jnp.where(lane < N_EXPERTS, logits, NEG_INF)
        mx = jnp.max(logits, axis=-1, keepdims=True)
        ex = jnp.exp(logits - mx)
        aff = ex / jnp.sum(ex, axis=-1, keepdims=True)
        xe_ref[:, d:] = aff
        afft_ref[...] = aff.T[:N_EXPERTS, :]


def _out_ln_router(oa, ob, oc, w_out, l, x, g, b, w_router_pad):
    n, d = x.shape
    tm, tn = 512, 512
    nj = d // tn
    wq = w_out.shape[1] // 4
    assert oa.shape[1] == 2 * wq and ob.shape[1] == wq and oc.shape[1] == wq

    def w_spec(r):
        return pl.BlockSpec((None, wq, tn), lambda i, j: (l, r, j))

    return pl.pallas_call(
        functools.partial(_out_ln_router_kernel, nj=nj, tn=tn),
        grid=(n // tm, nj),
        in_specs=[pl.BlockSpec((tm, oa.shape[1]), lambda i, j: (i, 0)),
                  pl.BlockSpec((tm, ob.shape[1]), lambda i, j: (i, 0)),
                  pl.BlockSpec((tm, oc.shape[1]), lambda i, j: (i, 0)),
                  w_spec(0), w_spec(1), w_spec(2), w_spec(3),
                  pl.BlockSpec((tm, tn), lambda i, j: (i, j)),
                  pl.BlockSpec((1, d), lambda i, j: (0, 0)),
                  pl.BlockSpec((1, d), lambda i, j: (0, 0)),
                  pl.BlockSpec((d, LANES), lambda i, j: (0, 0))],
        out_specs=[pl.BlockSpec((tm, d + LANES), lambda i, j: (i, 0)),
                   pl.BlockSpec((N_EXPERTS, tm), lambda i, j: (0, i))],
        out_shape=[jax.ShapeDtypeStruct((n, d + LANES), F32),
                   jax.ShapeDtypeStruct((N_EXPERTS, n), F32)],
        scratch_shapes=[pltpu.VMEM((nj, tm, tn), F32)],
        compiler_params=_cparams(("parallel", "arbitrary")),
        name="out_proj_ln_router",
    )(oa, ob, oc, w_out, w_out, w_out, w_out, x, g, b, w_router_pad)


def _route_kernel(aff_ref, idx_ref, incl_ref, incl8_ref, *, cap):
    e_cnt, n = aff_ref.shape
    nch = n // LANES
    bits = pltpu.bitcast(aff_ref[...], I32)

    def count(mask):
        return jnp.sum(jnp.where(mask, 1.0, 0.0), axis=1, keepdims=True)

    def search(it, v):
        cand = v | jnp.left_shift(jnp.int32(1), 30 - it)
        return jnp.where(count(bits >= cand) >= cap, cand, v)

    thr = lax.fori_loop(0, 31, search, jnp.zeros((e_cnt, 1), I32))
    gtf = jnp.where(bits > thr, 1.0, 0.0)
    eqf = jnp.where(bits == thr, 1.0, 0.0)
    need = cap - jnp.sum(gtf, axis=1, keepdims=True)

    r = lax.broadcasted_iota(I32, (LANES, LANES), 0)
    c = lax.broadcasted_iota(I32, (LANES, LANES), 1)
    tri = jnp.where(r <= c, 1.0, 0.0).astype(BF16)
    ones = jnp.ones((LANES, LANES), BF16)

    def prefix(maskf):
        carry = jnp.zeros((e_cnt, LANES), F32)
        for j in range(nch):
            sl = slice(j * LANES, (j + 1) * LANES)
            mj = maskf[:, sl].astype(BF16)
            incl_ref[:, sl] = jnp.dot(mj, tri, preferred_element_type=F32) + carry
            carry = carry + jnp.dot(mj, ones, preferred_element_type=F32)

    prefix(eqf)
    tie_rank = incl_ref[...] - eqf
    chosen = gtf + eqf * jnp.where(tie_rank < need, 1.0, 0.0)
    prefix(chosen)
    for e in range(e_cnt):
        incl8_ref[e] = jnp.broadcast_to(incl_ref[e:e + 1, :], (8, n))

    ones_row = jnp.ones((8, LANES), BF16)
    slot_iota = lax.broadcasted_iota(I32, (LANES, 1), 0)

    def compact(it, carry):
        e = it // (cap // LANES)
        sb = it % (cap // LANES)
        slot = (slot_iota + sb * LANES).astype(F32)
        acc = jnp.zeros((LANES, LANES), F32)
        for j in range(nch):
            blk = incl8_ref[e, :, j * LANES:(j + 1) * LANES]
            acc += jnp.where(jnp.concatenate([blk] * (LANES // 8), axis=0) <= slot, 1.0, 0.0)
        tot = lax.dot_general(ones_row, acc.astype(BF16), (((1,), (1,)), ((), ())), preferred_element_type=F32)
        idx_ref[e, sb] = tot.astype(I32)
        return carry

    lax.fori_loop(0, e_cnt * (cap // LANES), compact, 0)


def _route(aff_t, cap):
    e_cnt, n = aff_t.shape
    idx = pl.pallas_call(
        functools.partial(_route_kernel, cap=cap),
        out_shape=jax.ShapeDtypeStruct((e_cnt, cap // LANES, 8, LANES), I32),
        scratch_shapes=[pltpu.VMEM((e_cnt, n), F32), pltpu.VMEM((e_cnt, 8, n), F32)],
        compiler_params=pltpu.CompilerParams(vmem_limit_bytes=VMEM_LIMIT),
        name="route",
    )(aff_t)
    return idx[:, :, 0, :].reshape(e_cnt, 1, cap)


def _ffn_kernel(idx_ref, idxn_ref, x_hbm, *refs, cap, nf, tf, d, rps):
    wgu_refs, wd_refs = refs[0:4], refs[4:4 + W_STREAMS]
    y_ref, xe_ref, h_ref, gate_ref, stage_ref, sem = refs[4 + W_STREAMS:]
    e = pl.program_id(0)
    s = pl.program_id(1)
    ne = pl.num_programs(0)
    ns = pl.num_programs(1)
    nchunk = cap // rps
    cur = e % 2
    e_next = jnp.minimum(e + 1, ne - 1)
    lane = lax.broadcasted_iota(I32, (1, LANES), 1)

    def row_copy(row, slot, r):
        return pltpu.make_async_copy(x_hbm.at[pl.ds(row, 1), :], stage_ref.at[slot, pl.ds(r, 1), :], sem.at[slot])

    def land(chunk, slot, buf, expert):
        pltpu.make_async_copy(x_hbm.at[pl.ds(0, rps), :], stage_ref.at[slot], sem.at[slot]).wait()
        rows = pl.ds(pl.multiple_of(chunk * rps, rps), rps)
        xe_ref[buf, rows, :] = stage_ref[slot, :, 0:d].astype(BF16)
        aff = stage_ref[slot, :, d:d + LANES]
        gate_ref[buf, rows, :] = jnp.sum(jnp.where(lane == expert, aff, 0.0), axis=-1, keepdims=True)

    @pl.when((e == 0) & (s == 0))
    def _():
        def fetch(c, carry):
            slot = c % 2

            def start(r, carry2):
                row_copy(idx_ref[0, 0, c * rps + r], slot, r).start()
                return carry2
            lax.fori_loop(0, rps, start, 0)

            @pl.when(c < nchunk - 1)
            def _():
                land(c, slot, 0, 0)
            return carry
        lax.fori_loop(0, nchunk, fetch, 0)

    def step_gather():
        first = s == 0
        land(jnp.where(first, nchunk - 1, s - 1), (s + 1) % 2,
             jnp.where(first, cur, 1 - cur), jnp.where(first, e, e_next))
        slot = s % 2
        for r in range(rps):
            row_copy(idxn_ref[0, 0, s * rps + r], slot, r).start()

    @pl.when(s < nf)
    def _():
        step_gather()
        w = jnp.concatenate([r[...].astype(BF16) for r in wgu_refs], axis=1)
        gu = jnp.dot(xe_ref[cur], w, preferred_element_type=F32)
        g, u = gu[:, :tf], gu[:, tf:]
        hv = (g * (1.0 / (1.0 + jnp.exp(-g))) * u).astype(BF16)
        for c in range(nf):
            @pl.when(s == c)
            def _():
                h_ref[:, c * tf:(c + 1) * tf] = hv

    @pl.when(s >= nf)
    def _():
        step_gather()
        w = jnp.concatenate([r[...].astype(BF16) for r in wd_refs], axis=0)
        y = jnp.dot(h_ref[...], w, preferred_element_type=F32)
        y_ref[...] = y * gate_ref[cur]

    @pl.when((e == ne - 1) & (s == ns - 1))
    def _():
        pltpu.make_async_copy(x_hbm.at[pl.ds(0, rps), :], stage_ref.at[(nchunk - 1) % 2],
                              sem.at[(nchunk - 1) % 2]).wait()


def _expert_ffn(idx, x_ext, w_gate, w_up, w_down, l, cap):
    _, e_cnt, d, f = w_gate.shape
    tf, tn = 256, 512
    nf, nn = f // tf, d // tn
    rps = cap // (nf + nn)
    assert rps * (nf + nn) == cap and rps % 16 == 0 and (nf + nn) % 2 == 0

    def gu_spec(c):
        return pl.BlockSpec((None, None, d, tf // 2), lambda e, s: (l, e, 0, 2 * jnp.minimum(s, nf - 1) + c))

    def d_spec(r):
        return pl.BlockSpec((None, None, f // W_STREAMS, tn), lambda e, s: (l, e, r, jnp.maximum(s - nf, 0)))

    return pl.pallas_call(
        functools.partial(_ffn_kernel, cap=cap, nf=nf, tf=tf, d=d, rps=rps),
        grid=(e_cnt, nf + nn),
        in_specs=[pl.BlockSpec((1, 1, cap), lambda e, s: (e, 0, 0), memory_space=pltpu.SMEM),
                  pl.BlockSpec((1, 1, cap), lambda e, s: (jnp.minimum(e + 1, e_cnt - 1), 0, 0),
                               memory_space=pltpu.SMEM),
                  pl.BlockSpec(memory_space=pl.ANY)]
                 + [gu_spec(0), gu_spec(1)] * 2 + [d_spec(r) for r in range(W_STREAMS)],
        out_specs=pl.BlockSpec((cap, tn), lambda e, s: (e, jnp.maximum(s - nf, 0))),
        out_shape=jax.ShapeDtypeStruct((e_cnt * cap, d), F32),
        scratch_shapes=[pltpu.VMEM((2, cap, d), BF16),
                        pltpu.VMEM((cap, f), BF16),
                        pltpu.VMEM((2, cap, 1), F32),
                        pltpu.VMEM((2, rps, x_ext.shape[1]), F32),
                        pltpu.SemaphoreType.DMA((2,))],
        compiler_params=_cparams(("arbitrary", "arbitrary")),
        name="expert_ffn",
    )(idx, idx, x_ext, w_gate, w_gate, w_up, w_up, *([w_down] * W_STREAMS))


def _combine_kernel(idx_ref, y_hbm, acc_in_hbm, acc_hbm, buf_ref, ybuf_ref, gsem, ssem, ysem, *, cap):
    del acc_in_hbm
    e = pl.program_id(0)
    nchunk = cap // GATHER_ROWS

    def y_copy(c, slot):
        return pltpu.make_async_copy(y_hbm.at[pl.ds(e * cap + c * GATHER_ROWS, GATHER_ROWS), :],
                                     ybuf_ref.at[slot], ysem.at[slot])

    def row_in(row, slot, r):
        return pltpu.make_async_copy(acc_hbm.at[pl.ds(row, 1), :], buf_ref.at[slot, pl.ds(r, 1), :], gsem.at[slot])

    def row_out(row, slot, r):
        return pltpu.make_async_copy(buf_ref.at[slot, pl.ds(r, 1), :], acc_hbm.at[pl.ds(row, 1), :], ssem.at[slot])

    def for_rows(fn):
        def body(r, carry):
            fn(r)
            return carry
        lax.fori_loop(0, GATHER_ROWS, body, 0, unroll=8)

    def start_gather(c, slot):
        y_copy(c, slot).start()
        for_rows(lambda r: row_in(idx_ref[0, 0, c * GATHER_ROWS + r], slot, r).start())

    def wait_gather(c, slot):
        y_copy(c, slot).wait()
        pltpu.make_async_copy(acc_hbm.at[pl.ds(0, GATHER_ROWS), :], buf_ref.at[slot], gsem.at[slot]).wait()

    def start_scatter(c, slot):
        for_rows(lambda r: row_out(idx_ref[0, 0, c * GATHER_ROWS + r], slot, r).start())

    def wait_scatter(slot):
        pltpu.make_async_copy(buf_ref.at[slot], acc_hbm.at[pl.ds(0, GATHER_ROWS), :], ssem.at[slot]).wait()

    start_gather(0, 0)
    for c in range(nchunk):
        slot = c % 2
        if c + 1 < nchunk:
            if c >= 1:
                wait_scatter(1 - slot)
            start_gather(c + 1, 1 - slot)
        wait_gather(c, slot)
        buf_ref[slot] = buf_ref[slot] + ybuf_ref[slot]
        start_scatter(c, slot)
    if nchunk >= 2:
        wait_scatter(nchunk % 2)
    wait_scatter((nchunk - 1) % 2)


def _combine(idx, y, acc, cap):
    e_cnt = idx.shape[0]
    n, d = acc.shape
    return pl.pallas_call(
        functools.partial(_combine_kernel, cap=cap),
        grid=(e_cnt,),
        in_specs=[pl.BlockSpec((1, 1, cap), lambda e: (e, 0, 0), memory_space=pltpu.SMEM),
                  pl.BlockSpec(memory_space=pl.ANY),
                  pl.BlockSpec(memory_space=pl.ANY)],
        out_specs=pl.BlockSpec(memory_space=pl.ANY),
        out_shape=jax.ShapeDtypeStruct((n, d), F32),
        input_output_aliases={2: 0},
        scratch_shapes=[pltpu.VMEM((2, GATHER_ROWS, d), F32),
                        pltpu.VMEM((2, GATHER_ROWS, d), F32),
                        pltpu.SemaphoreType.DMA((2,)),
                        pltpu.SemaphoreType.DMA((2,)),
                        pltpu.SemaphoreType.DMA((2,))],
        compiler_params=_cparams(("arbitrary",)),
        name="combine",
    )(idx, y, acc)


def _ln2_kernel(x_ref, s_ref, g_ref, b_ref, o_ref, ob_ref):
    pre = ALPHA * x_ref[...] + s_ref[...]
    mu = jnp.mean(pre, axis=-1, keepdims=True)
    var = jnp.mean(jnp.square(pre - mu), axis=-1, keepdims=True)
    y = (pre - mu) * lax.rsqrt(var + LN_EPS) * g_ref[...] + b_ref[...]
    o_ref[...] = y
    ob_ref[...] = y.astype(BF16)


def _ln2(x_ext, moe, g, b):
    n, d = moe.shape
    tm = 256
    return pl.pallas_call(
        _ln2_kernel,
        grid=(n // tm,),
        in_specs=[pl.BlockSpec((tm, d), lambda i: (i, 0)),
                  pl.BlockSpec((tm, d), lambda i: (i, 0)),
                  pl.BlockSpec((1, d), lambda i: (0, 0)),
                  pl.BlockSpec((1, d), lambda i: (0, 0))],
        out_specs=[pl.BlockSpec((tm, d), lambda i: (i, 0)),
                   pl.BlockSpec((tm, d), lambda i: (i, 0))],
        out_shape=[jax.ShapeDtypeStruct((n, d), F32), jax.ShapeDtypeStruct((n, d), BF16)],
        compiler_params=_cparams(("parallel",)),
        name="ln2",
    )(x_ext, moe, g, b)


def _layer(x, xb, bsz, t, l, lam_init, shared, p):
    n = bsz * t
    cap = EC_CAPACITY_FACTOR * n // N_EXPERTS
    h = _in_proj(xb, shared["w_in"], l)
    oa = _attn_a(h, shared["tab_a"], p["sink_col"], bsz, t)
    ob = _attn_b(h, p["tab_b"], bsz, t)
    oc = _attn_c(h, shared["tab_c"], p["lam"], p["subln"], bsz, t, lam_init)
    x_ext, aff_t = _out_ln_router(oa, ob, oc, shared["w_out"], l, x, p["ln1_g"], p["ln1_b"], p["w_router"])
    idx = _route(aff_t, cap)
    y = _expert_ffn(idx, x_ext, shared["w_gate"], shared["w_up"], shared["w_down"], l, cap)
    moe = _combine(idx, y, jnp.zeros((n, D_MODEL), F32), cap)
    return _ln2(x_ext, moe, p["ln2_g"], p["ln2_b"])


def _trunk(x3, shared, layers):
    bsz, t, d = x3.shape
    x = x3.reshape(bsz * t, d)
    xb = x.astype(BF16)
    for l in range(DEPTH):
        lam_init = 0.8 - 0.6 * math.exp(-0.3 * l)
        x, xb = _layer(x, xb, bsz, t, l, lam_init, shared, layers[l])
    return x.reshape(bsz, t, d)


def kernel(x_prompt, x_sample, w_in, w_out, ln1_g, ln1_b, ln2_g, ln2_b, sink_a, rpb_b, diff_lambda, subln_c,
           t5_table, w_router, w_gate, w_up, w_down):
    shared = dict(
        tab_a=_table_a(t5_table[:, :N_HEADS_A]), tab_c=_table_c(t5_table[:, N_HEADS_A:]),
        w_in=w_in.astype(BF16), w_out=w_out.astype(BF16), w_gate=w_gate, w_up=w_up, w_down=w_down)
    layers = []
    for l in range(DEPTH):
        lam_init = 0.8 - 0.6 * math.exp(-0.3 * l)
        lf = diff_lambda[l].astype(F32)
        lam = jnp.exp(jnp.sum(lf[0] * lf[1])) - jnp.exp(jnp.sum(lf[2] * lf[3])) + lam_init
        sink_col = jnp.broadcast_to(jnp.repeat(sink_a[l].astype(F32), BLOCK).reshape(N_KV_A, GQA_GROUP * BLOCK, 1),
                                    (N_KV_A, GQA_GROUP * BLOCK, LANES))
        layers.append(dict(
            ln1_g=ln1_g[l][None], ln1_b=ln1_b[l][None], ln2_g=ln2_g[l][None], ln2_b=ln2_b[l][None],
            sink_col=sink_col, tab_b=_table_b(rpb_b[l]), lam=lam[None].astype(F32),
            subln=subln_c[l][None].astype(F32),
            w_router=jnp.pad(w_router[l], ((0, 0), (0, LANES - N_EXPERTS)))))
    return (_trunk(x_prompt, shared, layers), _trunk(x_sample, shared, layers))
```

```python
import functools
import math

import numpy as np
import jax
import jax.numpy as jnp
from jax import lax
from jax.experimental import pallas as pl
from jax.experimental.pallas import tpu as pltpu

F32 = jnp.float32
BF16 = jnp.bfloat16
I32 = jnp.int32

D_MODEL = 4096
DEPTH = 4
HEAD_DIM = 128
N_HEADS_A = 16
N_KV_A = 4
GQA_GROUP = 4
WINDOW = 128
BLOCK = 128
N_HEADS_B = 8
GRID_W = 64
NA_ROWS = 8
NA_COLS = 16
N_HEADS_C = 8
DIFF_QK_DIM = 64
T5_BUCKETS = 32
T5_MAX_DIST = 128
N_EXPERTS = 16
EC_CAPACITY_FACTOR = 2
D_EXPERT = 2048
LN_EPS = 1e-5
NEG_INF = -1e30
ALPHA = (2 * DEPTH) ** 0.25
IN_COLS = 9216

QA_CB, KA_CB, VA_CB = 0, 16, 20
QB_CB, KB_CB, VB_CB = 24, 32, 40
QC_CB, KC_CB, VC_CB = 48, 56, 64

LANES = 128
EXT_COLS = D_MODEL + LANES
VMEM_LIMIT = 56 * 1024 * 1024

QT = 256
KWIN = 768
KT = 512
GATHER_ROWS = 256
GATHER_PRIORITY = 1


def _cparams(sem):
    return pltpu.CompilerParams(dimension_semantics=sem, vmem_limit_bytes=VMEM_LIMIT)


W_STREAMS = 4


def _mm_kernel(x_ref, *refs):
    w_refs, o_ref = refs[:-1], refs[-1]
    x = x_ref[...]
    tn = w_refs[0].shape[1]
    for c, w_ref in enumerate(w_refs):
        o_ref[:, c * tn:(c + 1) * tn] = jnp.dot(x, w_ref[...], preferred_element_type=F32).astype(o_ref.dtype)


def _in_proj(xb, w, l):
    n, d = xb.shape
    cols = w.shape[2]
    tm, tn = 1024, 1024
    ts = tn // W_STREAMS

    def w_spec(c):
        return pl.BlockSpec((None, d, ts), lambda i, j: (l, 0, j * W_STREAMS + c))

    return pl.pallas_call(
        _mm_kernel,
        grid=(n // tm, cols // tn),
        in_specs=[pl.BlockSpec((tm, d), lambda i, j: (i, 0))] + [w_spec(c) for c in range(W_STREAMS)],
        out_specs=pl.BlockSpec((tm, tn), lambda i, j: (i, j)),
        out_shape=jax.ShapeDtypeStruct((n, cols), BF16),
        compiler_params=_cparams(("parallel", "arbitrary")),
        name="in_proj",
    )(xb, *([w] * W_STREAMS))


def _t5_bucket(rel):
    half = T5_BUCKETS // 2
    max_exact = half // 2
    n = jnp.abs(rel)
    n_f = jnp.maximum(n, 1).astype(F32)
    large = max_exact + (jnp.log(n_f / max_exact) / math.log(T5_MAX_DIST / max_exact) * (half - max_exact)).astype(I32)
    large = jnp.minimum(large, half - 1)
    return jnp.where(rel > 0, half, 0) + jnp.where(n < max_exact, n, large)


def _t5_lookup(table, rel):
    onehot = (_t5_bucket(rel)[..., None] == jnp.arange(T5_BUCKETS)).astype(F32)
    return jnp.einsum('...b,bh->h...', onehot, table.astype(F32), precision=lax.Precision.HIGHEST)


def _table_a(t5_tab_a):
    rel = jnp.arange(3 * BLOCK)[None, :] - BLOCK - jnp.arange(BLOCK)[:, None]
    bias = jnp.where((jnp.abs(rel) <= WINDOW)[None], _t5_lookup(t5_tab_a, rel), NEG_INF)
    return bias.reshape(N_KV_A, GQA_GROUP * BLOCK, 3 * BLOCK)


def _table_c(t5_tab_c):
    qq = jnp.arange(QT)[:, None]
    kk = jnp.arange(KT)[None, :]
    rel = jnp.stack([QT * (v - 3) + kk - qq for v in range(1, 5)])
    near = _t5_lookup(t5_tab_c, rel)
    far = _t5_lookup(t5_tab_c, jnp.array([-(KT + QT), KT + QT]))
    far = jnp.broadcast_to(far[:, :, None, None], (N_HEADS_C, 2, QT, KT))
    return jnp.concatenate([far[:, 0:1], near, far[:, 1:2]], axis=1)


def _table_b_static():
    rows_q, rows_k = QT // GRID_W, KWIN // GRID_W
    qc = np.arange(GRID_W)[:, None]
    kc = np.arange(GRID_W)[None, :]
    col_start = np.clip(qc - NA_COLS // 2, 0, GRID_W - NA_COLS)
    col_valid = (kc >= col_start) & (kc < col_start + NA_COLS)
    dc = np.clip(kc - qc, -(NA_COLS - 1), NA_COLS - 1) + NA_COLS - 1
    oh_c = (dc[None] == np.arange(2 * NA_COLS - 1)[:, None, None]).astype(np.float32)
    ty = np.arange(3)[:, None, None]
    ql = np.arange(rows_q)[None, :, None]
    kl = np.arange(rows_k)[None, None, :]
    dr = kl - rows_q * ty - ql + NA_ROWS - 1
    win_lo = np.where(ty == 0, 0, np.where(ty == 1, ql, rows_k - NA_ROWS))
    win = (kl >= win_lo) & (kl < win_lo + NA_ROWS)
    oh_r = ((dr[None] == np.arange(2 * NA_ROWS - 1)[:, None, None, None]) & win[None]).astype(np.float32)
    valid = win[:, :, None, :, None] & col_valid[None, None, :, None, :]
    return oh_c, oh_r, valid.reshape(3, QT, KWIN)


_TB_OH_C, _TB_OH_R, _TB_VALID = _table_b_static()


def _table_b(rpb):
    hp = lax.Precision.HIGHEST
    cols = jnp.einsum('hrc,cqk->hrqk', rpb.astype(F32), _TB_OH_C, precision=hp)
    bias = jnp.einsum('hrqk,rtal->htaqlk', cols, _TB_OH_R, precision=hp).reshape(N_HEADS_B, 3, QT, KWIN)
    return jnp.where(_TB_VALID[None], bias, NEG_INF)


def _attn_a_kernel(q_ref, kp_ref, kc_ref, kn_ref, vp_ref, vc_ref, vn_ref, tab_ref, sink_ref, o_ref, *, npair):
    i = pl.program_id(2)
    k4 = jnp.concatenate([kp_ref[...], kc_ref[...], kn_ref[...]], axis=0)
    v4 = jnp.concatenate([vp_ref[...], vc_ref[...], vn_ref[...]], axis=0)
    col = lax.broadcasted_iota(I32, (1, 3 * BLOCK), 1)
    sk = sink_ref[0]
    ones = jnp.ones((3 * BLOCK, HEAD_DIM), BF16)
    for u in range(2):
        q = q_ref[u * BLOCK:(u + 1) * BLOCK, :]
        q4 = jnp.concatenate([q[:, c * HEAD_DIM:(c + 1) * HEAD_DIM] for c in range(GQA_GROUP)], axis=0)
        k3 = k4[u * BLOCK:(u + 3) * BLOCK]
        v3 = v4[u * BLOCK:(u + 3) * BLOCK]
        s = lax.dot_general(q4, k3, (((1,), (1,)), ((), ())), preferred_element_type=F32)
        s = s * (HEAD_DIM ** -0.5) + tab_ref[0]
        if u == 0:
            lo, hi = jnp.where(i == 0, BLOCK, 0), 3 * BLOCK
        else:
            lo, hi = 0, jnp.where(i == npair - 1, 2 * BLOCK, 3 * BLOCK)
        s = jnp.where((col < lo) | (col >= hi), NEG_INF, s)
        m = jnp.maximum(jnp.max(s, axis=-1, keepdims=True), sk)
        p = jnp.exp(s - jnp.concatenate([m] * 3, axis=1))
        pv = jnp.dot(p.astype(BF16), jnp.concatenate([v3, ones], axis=1), preferred_element_type=F32)
        o = pv[:, :HEAD_DIM] / (pv[:, HEAD_DIM:] + jnp.exp(sk - m))
        o_ref[u * BLOCK:(u + 1) * BLOCK, :] = jnp.concatenate(
            [o[c * BLOCK:(c + 1) * BLOCK] for c in range(GQA_GROUP)], axis=1).astype(o_ref.dtype)


def _attn_a(h, tab_a, sink_col, bsz, t):
    nb = t // BLOCK
    npair = nb // 2
    n = bsz * t

    def edge_spec(cb, blk):
        return pl.BlockSpec((BLOCK, HEAD_DIM), lambda b, g, i: (b * nb + jnp.clip(blk(i), 0, nb - 1), cb + g))

    def pair_spec(cb):
        return pl.BlockSpec((2 * BLOCK, HEAD_DIM), lambda b, g, i: (b * npair + i, cb + g))

    return pl.pallas_call(
        functools.partial(_attn_a_kernel, npair=npair),
        grid=(bsz, N_KV_A, npair),
        in_specs=[pl.BlockSpec((2 * BLOCK, GQA_GROUP * HEAD_DIM), lambda b, g, i: (b * npair + i, g)),
                  edge_spec(KA_CB, lambda i: 2 * i - 1), pair_spec(KA_CB), edge_spec(KA_CB, lambda i: 2 * i + 2),
                  edge_spec(VA_CB, lambda i: 2 * i - 1), pair_spec(VA_CB), edge_spec(VA_CB, lambda i: 2 * i + 2),
                  pl.BlockSpec((1, GQA_GROUP * BLOCK, 3 * BLOCK), lambda b, g, i: (g, 0, 0)),
                  pl.BlockSpec((1, GQA_GROUP * BLOCK, LANES), lambda b, g, i: (g, 0, 0))],
        out_specs=pl.BlockSpec((2 * BLOCK, GQA_GROUP * HEAD_DIM), lambda b, g, i: (b * npair + i, g)),
        out_shape=jax.ShapeDtypeStruct((n, N_HEADS_A * HEAD_DIM), BF16),
        compiler_params=_cparams(("parallel", "parallel", "arbitrary")),
        name="attn_window",
    )(h, h, h, h, h, h, h, tab_a, sink_col)


def _attn_b_kernel(q_ref, k0_ref, k1_ref, k2_ref, v0_ref, v1_ref, v2_ref, tab_ref, o_ref):
    k3 = jnp.concatenate([k0_ref[...], k1_ref[...], k2_ref[...]], axis=0)
    v3 = jnp.concatenate([v0_ref[...], v1_ref[...], v2_ref[...]], axis=0)
    ones = jnp.ones((KWIN, HEAD_DIM), BF16)
    for u in range(2):
        cols = slice(u * HEAD_DIM, (u + 1) * HEAD_DIM)
        s = lax.dot_general(q_ref[:, cols], k3[:, cols], (((1,), (1,)), ((), ())), preferred_element_type=F32)
        s = s * (HEAD_DIM ** -0.5) + tab_ref[u, 0]
        p = jnp.exp(s - jnp.max(s, axis=-1, keepdims=True))
        pv = jnp.dot(p.astype(BF16), jnp.concatenate([v3[:, cols], ones], axis=1), preferred_element_type=F32)
        o_ref[:, cols] = (pv[:, :HEAD_DIM] / pv[:, HEAD_DIM:]).astype(o_ref.dtype)


def _attn_b(h, tab_b, bsz, t):
    nq = t // QT
    n = bsz * t

    def kv_spec(cb, j):
        return pl.BlockSpec((QT, 2 * HEAD_DIM),
                            lambda b, hp, g: (b * nq + jnp.clip(g - 1, 0, nq - 3) + j, cb // 2 + hp))

    def tab_map(b, hp, g):
        ty = jnp.where(g == 0, 0, jnp.where(g == nq - 1, 2, 1))
        return (hp, ty, 0, 0)

    return pl.pallas_call(
        _attn_b_kernel,
        grid=(bsz, N_HEADS_B // 2, nq),
        in_specs=[pl.BlockSpec((QT, 2 * HEAD_DIM), lambda b, hp, g: (b * nq + g, QB_CB // 2 + hp)),
                  kv_spec(KB_CB, 0), kv_spec(KB_CB, 1), kv_spec(KB_CB, 2),
                  kv_spec(VB_CB, 0), kv_spec(VB_CB, 1), kv_spec(VB_CB, 2),
                  pl.BlockSpec((2, 1, QT, KWIN), tab_map)],
        out_specs=pl.BlockSpec((QT, 2 * HEAD_DIM), lambda b, hp, g: (b * nq + g, hp)),
        out_shape=jax.ShapeDtypeStruct((n, N_HEADS_B * HEAD_DIM), BF16),
        compiler_params=_cparams(("parallel", "parallel", "arbitrary")),
        name="attn_nbr",
    )(h, h, h, h, h, h, h, tab_b)


def _attn_c_kernel(lam_ref, q_ref, k_ref, v_ref, tab_ref, g_ref, o_ref,
                   vaug_ref, m_ref, acc_ref, *, t, lam_init):
    i = pl.program_id(2)

    @pl.when(i == 0)
    def _():
        vaug_ref[:, :HEAD_DIM] = v_ref[...]
        vaug_ref[:, HEAD_DIM:] = jnp.ones((t, HEAD_DIM), BF16)

    q = q_ref[...] * jnp.asarray(DIFF_QK_DIM ** -0.5, BF16)
    qs = (q[:, :DIFF_QK_DIM], q[:, DIFF_QK_DIM:])
    m_ref[...] = jnp.full(m_ref.shape, NEG_INF, F32)
    acc_ref[...] = jnp.zeros(acc_ref.shape, F32)

    for j in range(t // KT):
        keys = slice(j * KT, (j + 1) * KT)
        bias = tab_ref[0, jnp.clip(2 * j - i + 3, 0, 5)]
        kt = k_ref[keys, :]
        for c in range(2):
            kc = kt[:, c * DIFF_QK_DIM:(c + 1) * DIFF_QK_DIM]
            s = lax.dot_general(qs[c], kc, (((1,), (1,)), ((), ())), preferred_element_type=F32) + bias
            m_old = m_ref[c]
            m_new = jnp.maximum(m_old, jnp.max(s, axis=-1, keepdims=True))
            a = jnp.exp(m_old - m_new)
            p = jnp.exp(s - jnp.concatenate([m_new] * (KT // LANES), axis=1))
            pv = jnp.dot(p.astype(BF16), vaug_ref[keys, :], preferred_element_type=F32)
            acc_ref[c] = jnp.concatenate([a, a], axis=1) * acc_ref[c] + pv
            m_ref[c] = m_new

    def normalised(c):
        acc = acc_ref[c]
        return acc[:, :HEAD_DIM] / acc[:, HEAD_DIM:]

    o = normalised(0) - lam_ref[0] * normalised(1)
    o = o * lax.rsqrt(jnp.mean(jnp.square(o), axis=-1, keepdims=True) + LN_EPS) * g_ref[...] * (1.0 - lam_init)
    o_ref[...] = o.astype(o_ref.dtype)


def _attn_c(h, tab_c, lam, subln_g, bsz, t, lam_init):
    nq = t // QT
    n = bsz * t
    grid_spec = pltpu.PrefetchScalarGridSpec(
        num_scalar_prefetch=1,
        grid=(bsz, N_HEADS_C, nq),
        in_specs=[pl.BlockSpec((QT, HEAD_DIM), lambda b, hd, i, s: (b * nq + i, QC_CB + hd)),
                  pl.BlockSpec((t, HEAD_DIM), lambda b, hd, i, s: (b, KC_CB + hd)),
                  pl.BlockSpec((t, HEAD_DIM), lambda b, hd, i, s: (b, VC_CB + hd)),
                  pl.BlockSpec((1, 6, QT, KT), lambda b, hd, i, s: (hd, 0, 0, 0)),
                  pl.BlockSpec((1, HEAD_DIM), lambda b, hd, i, s: (0, 0))],
        out_specs=pl.BlockSpec((QT, HEAD_DIM), lambda b, hd, i, s: (b * nq + i, hd)),
        scratch_shapes=[pltpu.VMEM((t, 2 * HEAD_DIM), BF16),
                        pltpu.VMEM((2, QT, LANES), F32),
                        pltpu.VMEM((2, QT, 2 * HEAD_DIM), F32)],
    )
    return pl.pallas_call(
        functools.partial(_attn_c_kernel, t=t, lam_init=lam_init),
        grid_spec=grid_spec,
        out_shape=jax.ShapeDtypeStruct((n, N_HEADS_C * HEAD_DIM), BF16),
        compiler_params=_cparams(("parallel", "parallel", "arbitrary")),
        name="attn_diff",
    )(lam, h, h, h, tab_c, subln_g)


def _split_bf16(x):
    hi = x.astype(BF16)
    lo = (x - hi.astype(F32)).astype(BF16)
    return hi, lo


def _out_ln_router_kernel(oa_ref, ob_ref, oc_ref, w0_ref, w1_ref, w2_ref, w3_ref, x_ref, g_ref, b_ref, wr_ref,
                          xe_ref, afft_ref, pre_ref, *, nj, tn):
    j = pl.program_id(1)
    q = w0_ref.shape[0]
    acc = jnp.dot(oa_ref[:, 0:q], w0_ref[...], preferred_element_type=F32)
    acc += jnp.dot(oa_ref[:, q:2 * q], w1_ref[...], preferred_element_type=F32)
    acc += jnp.dot(ob_ref[...], w2_ref[...], preferred_element_type=F32)
    acc += jnp.dot(oc_ref[...], w3_ref[...], preferred_element_type=F32)
    pre_ref[j] = ALPHA * x_ref[...] + acc

    @pl.when(j == nj - 1)
    def _():
        tm = pre_ref.shape[1]
        d = nj * tn
        tot = jnp.zeros((tm, 1), F32)
        for c in range(nj):
            tot += jnp.sum(pre_ref[c], axis=-1, keepdims=True)
        mu = tot / d
        sq = jnp.zeros((tm, 1), F32)
        for c in range(nj):
            sq += jnp.sum(jnp.square(pre_ref[c] - mu), axis=-1, keepdims=True)
        rstd = lax.rsqrt(sq / d + LN_EPS)
        logits = jnp.zeros((tm, LANES), F32)
        for c in range(nj):
            sl = slice(c * tn, (c + 1) * tn)
            y = (pre_ref[c] - mu) * rstd * g_ref[:, sl] + b_ref[:, sl]
            xe_ref[:, sl] = y
            yh, yl = _split_bf16(y)
            wh, wl = _split_bf16(wr_ref[sl, :])
            logits += jnp.dot(yh, wh, preferred_element_type=F32)
            logits += jnp.dot(yh, wl, preferred_element_type=F32)
            logits += jnp.dot(yl, wh, preferred_element_type=F32)
        lane = lax.broadcasted_iota(I32, (1, LANES), 1)
        logits = jnp.where(lane < N_EXPERTS, logits, NEG_INF)
        mx = jnp.max(logits, axis=-1, keepdims=True)
        ex = jnp.exp(logits - mx)
        aff = ex / jnp.sum(ex, axis=-1, keepdims=True)
        xe_ref[:, d:] = aff
        afft_ref[...] = aff.T[:N_EXPERTS, :]


def _out_ln_router(oa, ob, oc, w_out, l, x, g, b, w_router_pad):
    n, d = x.shape
    tm, tn = 512, 512
    nj = d // tn
    wq = w_out.shape[1] // 4
    assert oa.shape[1] == 2 * wq and ob.shape[1] == wq and oc.shape[1] == wq

    def w_spec(r):
        return pl.BlockSpec((None, wq, tn), lambda i, j: (l, r, j))

    return pl.pallas_call(
        functools.partial(_out_ln_router_kernel, nj=nj, tn=tn),
        grid=(n // tm, nj),
        in_specs=[pl.BlockSpec((tm, oa.shape[1]), lambda i, j: (i, 0)),
                  pl.BlockSpec((tm, ob.shape[1]), lambda i, j: (i, 0)),
                  pl.BlockSpec((tm, oc.shape[1]), lambda i, j: (i, 0)),
                  w_spec(0), w_spec(1), w_spec(2), w_spec(3),
                  pl.BlockSpec((tm, tn), lambda i, j: (i, j)),
                  pl.BlockSpec((1, d), lambda i, j: (0, 0)),
                  pl.BlockSpec((1, d), lambda i, j: (0, 0)),
                  pl.BlockSpec((d, LANES), lambda i, j: (0, 0))],
        out_specs=[pl.BlockSpec((tm, d + LANES), lambda i, j: (i, 0)),
                   pl.BlockSpec((N_EXPERTS, tm), lambda i, j: (0, i))],
        out_shape=[jax.ShapeDtypeStruct((n, d + LANES), F32),
                   jax.ShapeDtypeStruct((N_EXPERTS, n), F32)],
        scratch_shapes=[pltpu.VMEM((nj, tm, tn), F32)],
        compiler_params=_cparams(("parallel", "arbitrary")),
        name="out_proj_ln_router",
    )(oa, ob, oc, w_out, w_out, w_out, w_out, x, g, b, w_router_pad)


def _route_kernel(aff_ref, idx_ref, incl_ref, incl8_ref, *, cap):
    e_cnt, n = aff_ref.shape
    nch = n // LANES
    bits = pltpu.bitcast(aff_ref[...], I32)

    def count(mask):
        return jnp.sum(jnp.where(mask, 1.0, 0.0), axis=1, keepdims=True)

    def search(it, v):
        cand = v | jnp.left_shift(jnp.int32(1), 30 - it)
        return jnp.where(count(bits >= cand) >= cap, cand, v)

    thr = lax.fori_loop(0, 31, search, jnp.zeros((e_cnt, 1), I32))
    gtf = jnp.where(bits > thr, 1.0, 0.0)
    eqf = jnp.where(bits == thr, 1.0, 0.0)
    need = cap - jnp.sum(gtf, axis=1, keepdims=True)

    r = lax.broadcasted_iota(I32, (LANES, LANES), 0)
    c = lax.broadcasted_iota(I32, (LANES, LANES), 1)
    tri = jnp.where(r <= c, 1.0, 0.0).astype(BF16)
    ones = jnp.ones((LANES, LANES), BF16)

    def prefix(maskf):
        carry = jnp.zeros((e_cnt, LANES), F32)
        for j in range(nch):
            sl = slice(j * LANES, (j + 1) * LANES)
            mj = maskf[:, sl].astype(BF16)
            incl_ref[:, sl] = jnp.dot(mj, tri, preferred_element_type=F32) + carry
            carry = carry + jnp.dot(mj, ones, preferred_element_type=F32)

    prefix(eqf)
    tie_rank = incl_ref[...] - eqf
    chosen = gtf + eqf * jnp.where(tie_rank < need, 1.0, 0.0)
    prefix(chosen)
    for e in range(e_cnt):
        incl8_ref[e] = jnp.broadcast_to(incl_ref[e:e + 1, :], (8, n))

    ones_row = jnp.ones((8, LANES), BF16)
    slot_iota = lax.broadcasted_iota(I32, (LANES, 1), 0)

    def compact(it, carry):
        e = it // (cap // LANES)
        sb = it % (cap // LANES)
        slot = (slot_iota + sb * LANES).astype(F32)
        acc = jnp.zeros((LANES, LANES), F32)
        for j in range(nch):
            blk = incl8_ref[e, :, j * LANES:(j + 1) * LANES]
            acc += jnp.where(jnp.concatenate([blk] * (LANES // 8), axis=0) <= slot, 1.0, 0.0)
        tot = lax.dot_general(ones_row, acc.astype(BF16), (((1,), (1,)), ((), ())), preferred_element_type=F32)
        idx_ref[e, sb] = tot.astype(I32)
        return carry

    lax.fori_loop(0, e_cnt * (cap // LANES), compact, 0)


def _route(aff_t, cap):
    e_cnt, n = aff_t.shape
    idx = pl.pallas_call(
        functools.partial(_route_kernel, cap=cap),
        out_shape=jax.ShapeDtypeStruct((e_cnt, cap // LANES, 8, LANES), I32),
        scratch_shapes=[pltpu.VMEM((e_cnt, n), F32), pltpu.VMEM((e_cnt, 8, n), F32)],
        compiler_params=pltpu.CompilerParams(vmem_limit_bytes=VMEM_LIMIT),
        name="route",
    )(aff_t)
    return idx[:, :, 0, :].reshape(e_cnt, 1, cap)


def _ffn_kernel(idx_ref, idxn_ref, x_hbm, *refs, cap, nf, tf, d, rps):
    wgu_refs, wd_refs = refs[0:4], refs[4:4 + W_STREAMS]
    y_ref, xe_ref, h_ref, gate_ref, stage_ref, sem = refs[4 + W_STREAMS:]
    e = pl.program_id(0)
    s = pl.program_id(1)
    ne = pl.num_programs(0)
    ns = pl.num_programs(1)
    nchunk = cap // rps
    cur = e % 2
    e_next = jnp.minimum(e + 1, ne - 1)
    lane = lax.broadcasted_iota(I32, (1, LANES), 1)

    def row_copy(row, slot, r):
        return pltpu.make_async_copy(x_hbm.at[pl.ds(row, 1), :], stage_ref.at[slot, pl.ds(r, 1), :], sem.at[slot])

    def land(chunk, slot, buf, expert):
        pltpu.make_async_copy(x_hbm.at[pl.ds(0, rps), :], stage_ref.at[slot], sem.at[slot]).wait()
        rows = pl.ds(pl.multiple_of(chunk * rps, rps), rps)
        xe_ref[buf, rows, :] = stage_ref[slot, :, 0:d].astype(BF16)
        aff = stage_ref[slot, :, d:d + LANES]
        gate_ref[buf, rows, :] = jnp.sum(jnp.where(lane == expert, aff, 0.0), axis=-1, keepdims=True)

    @pl.when((e == 0) & (s == 0))
    def _():
        def fetch(c, carry):
            slot = c % 2

            def start(r, carry2):
                row_copy(idx_ref[0, 0, c * rps + r], slot, r).start(priority=GATHER_PRIORITY)
                return carry2
            lax.fori_loop(0, rps, start, 0)

            @pl.when(c < nchunk - 1)
            def _():
                land(c, slot, 0, 0)
            return carry
        lax.fori_loop(0, nchunk, fetch, 0)

    def step_gather():
        first = s == 0
        land(jnp.where(first, nchunk - 1, s - 1), (s + 1) % 2,
             jnp.where(first, cur, 1 - cur), jnp.where(first, e, e_next))
        slot = s % 2
        for r in range(rps):
            row_copy(idxn_ref[0, 0, s * rps + r], slot, r).start(priority=GATHER_PRIORITY)

    @pl.when(s < nf)
    def _():
        step_gather()
        w = jnp.concatenate([r[...].astype(BF16) for r in wgu_refs], axis=1)
        gu = jnp.dot(xe_ref[cur], w, preferred_element_type=F32)
        g, u = gu[:, :tf], gu[:, tf:]
        hv = (g * (1.0 / (1.0 + jnp.exp(-g))) * u).astype(BF16)
        for c in range(nf):
            @pl.when(s == c)
            def _():
                h_ref[:, c * tf:(c + 1) * tf] = hv

    @pl.when(s >= nf)
    def _():
        step_gather()
        w = jnp.concatenate([r[...].astype(BF16) for r in wd_refs], axis=0)
        y = jnp.dot(h_ref[...], w, preferred_element_type=F32)
        y_ref[...] = y * gate_ref[cur]

    @pl.when((e == ne - 1) & (s == ns - 1))
    def _():
        pltpu.make_async_copy(x_hbm.at[pl.ds(0, rps), :], stage_ref.at[(nchunk - 1) % 2],
                              sem.at[(nchunk - 1) % 2]).wait()


def _expert_ffn(idx, x_ext, w_gate, w_up, w_down, l, cap):
    _, e_cnt, d, f = w_gate.shape
    tf, tn = 256, 512
    nf, nn = f // tf, d // tn
    rps = cap // (nf + nn)
    assert rps * (nf + nn) == cap and rps % 16 == 0 and (nf + nn) % 2 == 0

    def gu_spec(c):
        return pl.BlockSpec((None, None, d, tf // 2), lambda e, s: (l, e, 0, 2 * jnp.minimum(s, nf - 1) + c))

    def d_spec(r):
        return pl.BlockSpec((None, None, f // W_STREAMS, tn), lambda e, s: (l, e, r, jnp.maximum(s - nf, 0)))

    return pl.pallas_call(
        functools.partial(_ffn_kernel, cap=cap, nf=nf, tf=tf, d=d, rps=rps),
        grid=(e_cnt, nf + nn),
        in_specs=[pl.BlockSpec((1, 1, cap), lambda e, s: (e, 0, 0), memory_space=pltpu.SMEM),
                  pl.BlockSpec((1, 1, cap), lambda e, s: (jnp.minimum(e + 1, e_cnt - 1), 0, 0),
                               memory_space=pltpu.SMEM),
                  pl.BlockSpec(memory_space=pl.ANY)]
                 + [gu_spec(0), gu_spec(1)] * 2 + [d_spec(r) for r in range(W_STREAMS)],
        out_specs=pl.BlockSpec((cap, tn), lambda e, s: (e, jnp.maximum(s - nf, 0))),
        out_shape=jax.ShapeDtypeStruct((e_cnt * cap, d), F32),
        scratch_shapes=[pltpu.VMEM((2, cap, d), BF16),
                        pltpu.VMEM((cap, f), BF16),
                        pltpu.VMEM((2, cap, 1), F32),
                        pltpu.VMEM((2, rps, x_ext.shape[1]), F32),
                        pltpu.SemaphoreType.DMA((2,))],
        compiler_params=_cparams(("arbitrary", "arbitrary")),
        name="expert_ffn",
    )(idx, idx, x_ext, w_gate, w_gate, w_up, w_up, *([w_down] * W_STREAMS))


def _combine_kernel(idx_ref, y_hbm, acc_in_hbm, acc_hbm, buf_ref, ybuf_ref, gsem, ssem, ysem, *, cap):
    del acc_in_hbm
    e = pl.program_id(0)
    nchunk = cap // GATHER_ROWS

    def y_copy(c, slot):
        return pltpu.make_async_copy(y_hbm.at[pl.ds(e * cap + c * GATHER_ROWS, GATHER_ROWS), :],
                                     ybuf_ref.at[slot], ysem.at[slot])

    def row_in(row, slot, r):
        return pltpu.make_async_copy(acc_hbm.at[pl.ds(row, 1), :], buf_ref.at[slot, pl.ds(r, 1), :], gsem.at[slot])

    def row_out(row, slot, r):
        return pltpu.make_async_copy(buf_ref.at[slot, pl.ds(r, 1), :], acc_hbm.at[pl.ds(row, 1), :], ssem.at[slot])

    def for_rows(fn):
        def body(r, carry):
            fn(r)
            return carry
        lax.fori_loop(0, GATHER_ROWS, body, 0, unroll=8)

    def start_gather(c, slot):
        y_copy(c, slot).start()
        for_rows(lambda r: row_in(idx_ref[0, 0, c * GATHER_ROWS + r], slot, r).start(priority=GATHER_PRIORITY))

    def wait_gather(c, slot):
        y_copy(c, slot).wait()
        pltpu.make_async_copy(acc_hbm.at[pl.ds(0, GATHER_ROWS), :], buf_ref.at[slot], gsem.at[slot]).wait()

    def start_scatter(c, slot):
        for_rows(lambda r: row_out(idx_ref[0, 0, c * GATHER_ROWS + r], slot, r).start())

    def wait_scatter(slot):
        pltpu.make_async_copy(buf_ref.at[slot], acc_hbm.at[pl.ds(0, GATHER_ROWS), :], ssem.at[slot]).wait()

    start_gather(0, 0)
    for c in range(nchunk):
        slot = c % 2
        if c + 1 < nchunk:
            if c >= 1:
                wait_scatter(1 - slot)
            start_gather(c + 1, 1 - slot)
        wait_gather(c, slot)
        buf_ref[slot] = buf_ref[slot] + ybuf_ref[slot]
        start_scatter(c, slot)
    if nchunk >= 2:
        wait_scatter(nchunk % 2)
    wait_scatter((nchunk - 1) % 2)


def _combine(idx, y, acc, cap):
    e_cnt = idx.shape[0]
    n, d = acc.shape
    return pl.pallas_call(
        functools.partial(_combine_kernel, cap=cap),
        grid=(e_cnt,),
        in_specs=[pl.BlockSpec((1, 1, cap), lambda e: (e, 0, 0), memory_space=pltpu.SMEM),
                  pl.BlockSpec(memory_space=pl.ANY),
                  pl.BlockSpec(memory_space=pl.ANY)],
        out_specs=pl.BlockSpec(memory_space=pl.ANY),
        out_shape=jax.ShapeDtypeStruct((n, d), F32),
        input_output_aliases={2: 0},
        scratch_shapes=[pltpu.VMEM((2, GATHER_ROWS, d), F32),
                        pltpu.VMEM((2, GATHER_ROWS, d), F32),
                        pltpu.SemaphoreType.DMA((2,)),
                        pltpu.SemaphoreType.DMA((2,)),
                        pltpu.SemaphoreType.DMA((2,))],
        compiler_params=_cparams(("arbitrary",)),
        name="combine",
    )(idx, y, acc)


def _ln2_kernel(x_ref, s_ref, g_ref, b_ref, o_ref, ob_ref):
    pre = ALPHA * x_ref[...] + s_ref[...]
    mu = jnp.mean(pre, axis=-1, keepdims=True)
    var = jnp.mean(jnp.square(pre - mu), axis=-1, keepdims=True)
    y = (pre - mu) * lax.rsqrt(var + LN_EPS) * g_ref[...] + b_ref[...]
    o_ref[...] = y
    ob_ref[...] = y.astype(BF16)


def _ln2(x_ext, moe, g, b):
    n, d = moe.shape
    tm = 256
    return pl.pallas_call(
        _ln2_kernel,
        grid=(n // tm,),
        in_specs=[pl.BlockSpec((tm, d), lambda i: (i, 0)),
                  pl.BlockSpec((tm, d), lambda i: (i, 0)),
                  pl.BlockSpec((1, d), lambda i: (0, 0)),
                  pl.BlockSpec((1, d), lambda i: (0, 0))],
        out_specs=[pl.BlockSpec((tm, d), lambda i: (i, 0)),
                   pl.BlockSpec((tm, d), lambda i: (i, 0))],
        out_shape=[jax.ShapeDtypeStruct((n, d), F32), jax.ShapeDtypeStruct((n, d), BF16)],
        compiler_params=_cparams(("parallel",)),
        name="ln2",
    )(x_ext, moe, g, b)


def _layer(x, xb, bsz, t, l, lam_init, shared, p):
    n = bsz * t
    cap = EC_CAPACITY_FACTOR * n // N_EXPERTS
    h = _in_proj(xb, shared["w_in"], l)
    oa = _attn_a(h, shared["tab_a"], p["sink_col"], bsz, t)
    ob = _attn_b(h, p["tab_b"], bsz, t)
    oc = _attn_c(h, shared["tab_c"], p["lam"], p["subln"], bsz, t, lam_init)
    x_ext, aff_t = _out_ln_router(oa, ob, oc, shared["w_out"], l, x, p["ln1_g"], p["ln1_b"], p["w_router"])
    idx = _route(aff_t, cap)
    y = _expert_ffn(idx, x_ext, shared["w_gate"], shared["w_up"], shared["w_down"], l, cap)
    moe = _combine(idx, y, jnp.zeros((n, D_MODEL), F32), cap)
    return _ln2(x_ext, moe, p["ln2_g"], p["ln2_b"])


def _trunk(x3, shared, layers):
    bsz, t, d = x3.shape
    x = x3.reshape(bsz * t, d)
    xb = x.astype(BF16)
    for l in range(DEPTH):
        lam_init = 0.8 - 0.6 * math.exp(-0.3 * l)
        x, xb = _layer(x, xb, bsz, t, l, lam_init, shared, layers[l])
    return x.reshape(bsz, t, d)


def kernel(x_prompt, x_sample, w_in, w_out, ln1_g, ln1_b, ln2_g, ln2_b, sink_a, rpb_b, diff_lambda, subln_c,
           t5_table, w_router, w_gate, w_up, w_down):
    shared = dict(
        tab_a=_table_a(t5_table[:, :N_HEADS_A]), tab_c=_table_c(t5_table[:, N_HEADS_A:]),
        w_in=w_in.astype(BF16), w_out=w_out.astype(BF16), w_gate=w_gate, w_up=w_up, w_down=w_down)
    layers = []
    for l in range(DEPTH):
        lam_init = 0.8 - 0.6 * math.exp(-0.3 * l)
        lf = diff_lambda[l].astype(F32)
        lam = jnp.exp(jnp.sum(lf[0] * lf[1])) - jnp.exp(jnp.sum(lf[2] * lf[3])) + lam_init
        sink_col = jnp.broadcast_to(jnp.repeat(sink_a[l].astype(F32), BLOCK).reshape(N_KV_A, GQA_GROUP * BLOCK, 1),
                                    (N_KV_A, GQA_GROUP * BLOCK, LANES))
        layers.append(dict(
            ln1_g=ln1_g[l][None], ln1_b=ln1_b[l][None], ln2_g=ln2_g[l][None], ln2_b=ln2_b[l][None],
            sink_col=sink_col, tab_b=_table_b(rpb_b[l]), lam=lam[None].astype(F32),
            subln=subln_c[l][None].astype(F32),
            w_router=jnp.pad(w_router[l], ((0, 0), (0, LANES - N_EXPERTS)))))
    return (_trunk(x_prompt, shared, layers), _trunk(x_sample, shared, layers))
```

```python
import functools
import math

import numpy as np
import jax
import jax.numpy as jnp
from jax import lax
from jax.experimental import pallas as pl
from jax.experimental.pallas import tpu as pltpu

F32 = jnp.float32
BF16 = jnp.bfloat16
I32 = jnp.int32

D_MODEL = 4096
DEPTH = 4
HEAD_DIM = 128
N_HEADS_A = 16
N_KV_A = 4
GQA_GROUP = 4
WINDOW = 128
BLOCK = 128
N_HEADS_B = 8
GRID_W = 64
NA_ROWS = 8
NA_COLS = 16
N_HEADS_C = 8
DIFF_QK_DIM = 64
T5_BUCKETS = 32
T5_MAX_DIST = 128
N_EXPERTS = 16
EC_CAPACITY_FACTOR = 2
D_EXPERT = 2048
LN_EPS = 1e-5
NEG_INF = -1e30
ALPHA = (2 * DEPTH) ** 0.25
IN_COLS = 9216

QA_CB, KA_CB, VA_CB = 0, 16, 20
QB_CB, KB_CB, VB_CB = 24, 32, 40
QC_CB, KC_CB, VC_CB = 48, 56, 64

LANES = 128
EXT_COLS = D_MODEL + LANES
VMEM_LIMIT = 56 * 1024 * 1024

QT = 256
KWIN = 768
KT = 512
GATHER_ROWS = 256


def _cparams(sem):
    return pltpu.CompilerParams(dimension_semantics=sem, vmem_limit_bytes=VMEM_LIMIT)


W_STREAMS = 4


def _mm_kernel(x_ref, *refs):
    w_refs, o_ref = refs[:-1], refs[-1]
    x = x_ref[...]
    tn = w_refs[0].shape[1]
    for c, w_ref in enumerate(w_refs):
        o_ref[:, c * tn:(c + 1) * tn] = jnp.dot(x, w_ref[...], preferred_element_type=F32).astype(o_ref.dtype)


def _in_proj(xb, w, l):
    n, d = xb.shape
    cols = w.shape[2]
    tm, tn = 1024, 1024
    ts = tn // W_STREAMS

    def w_spec(c):
        return pl.BlockSpec((None, d, ts), lambda i, j: (l, 0, j * W_STREAMS + c))

    return pl.pallas_call(
        _mm_kernel,
        grid=(n // tm, cols // tn),
        in_specs=[pl.BlockSpec((tm, d), lambda i, j: (i, 0))] + [w_spec(c) for c in range(W_STREAMS)],
        out_specs=pl.BlockSpec((tm, tn), lambda i, j: (i, j)),
        out_shape=jax.ShapeDtypeStruct((n, cols), BF16),
        compiler_params=_cparams(("parallel", "arbitrary")),
        name="in_proj",
    )(xb, *([w] * W_STREAMS))


def _t5_bucket(rel):
    half = T5_BUCKETS // 2
    max_exact = half // 2
    n = jnp.abs(rel)
    n_f = jnp.maximum(n, 1).astype(F32)
    large = max_exact + (jnp.log(n_f / max_exact) / math.log(T5_MAX_DIST / max_exact) * (half - max_exact)).astype(I32)
    large = jnp.minimum(large, half - 1)
    return jnp.where(rel > 0, half, 0) + jnp.where(n < max_exact, n, large)


def _t5_lookup(table, rel):
    onehot = (_t5_bucket(rel)[..., None] == jnp.arange(T5_BUCKETS)).astype(F32)
    return jnp.einsum('...b,bh->h...', onehot, table.astype(F32), precision=lax.Precision.HIGHEST)


def _table_a(t5_tab_a):
    rel = jnp.arange(3 * BLOCK)[None, :] - BLOCK - jnp.arange(BLOCK)[:, None]
    bias = jnp.where((jnp.abs(rel) <= WINDOW)[None], _t5_lookup(t5_tab_a, rel), NEG_INF)
    return bias.reshape(N_KV_A, GQA_GROUP * BLOCK, 3 * BLOCK)


def _table_c(t5_tab_c):
    qq = jnp.arange(QT)[:, None]
    kk = jnp.arange(KT)[None, :]
    rel = jnp.stack([QT * (v - 3) + kk - qq for v in range(1, 5)])
    near = _t5_lookup(t5_tab_c, rel)
    far = _t5_lookup(t5_tab_c, jnp.array([-(KT + QT), KT + QT]))
    far = jnp.broadcast_to(far[:, :, None, None], (N_HEADS_C, 2, QT, KT))
    return jnp.concatenate([far[:, 0:1], near, far[:, 1:2]], axis=1)


def _table_b_static():
    rows_q, rows_k = QT // GRID_W, KWIN // GRID_W
    qc = np.arange(GRID_W)[:, None]
    kc = np.arange(GRID_W)[None, :]
    col_start = np.clip(qc - NA_COLS // 2, 0, GRID_W - NA_COLS)
    col_valid = (kc >= col_start) & (kc < col_start + NA_COLS)
    dc = np.clip(kc - qc, -(NA_COLS - 1), NA_COLS - 1) + NA_COLS - 1
    oh_c = (dc[None] == np.arange(2 * NA_COLS - 1)[:, None, None]).astype(np.float32)
    ty = np.arange(3)[:, None, None]
    ql = np.arange(rows_q)[None, :, None]
    kl = np.arange(rows_k)[None, None, :]
    dr = kl - rows_q * ty - ql + NA_ROWS - 1
    win_lo = np.where(ty == 0, 0, np.where(ty == 1, ql, rows_k - NA_ROWS))
    win = (kl >= win_lo) & (kl < win_lo + NA_ROWS)
    oh_r = ((dr[None] == np.arange(2 * NA_ROWS - 1)[:, None, None, None]) & win[None]).astype(np.float32)
    valid = win[:, :, None, :, None] & col_valid[None, None, :, None, :]
    return oh_c, oh_r, valid.reshape(3, QT, KWIN)


_TB_OH_C, _TB_OH_R, _TB_VALID = _table_b_static()


def _table_b(rpb):
    hp = lax.Precision.HIGHEST
    cols = jnp.einsum('hrc,cqk->hrqk', rpb.astype(F32), _TB_OH_C, precision=hp)
    bias = jnp.einsum('hrqk,rtal->htaqlk', cols, _TB_OH_R, precision=hp).reshape(N_HEADS_B, 3, QT, KWIN)
    return jnp.where(_TB_VALID[None], bias, NEG_INF)


def _attn_a_kernel(q_ref, kp_ref, kc_ref, kn_ref, vp_ref, vc_ref, vn_ref, tab_ref, sink_ref, o_ref, *, npair):
    i = pl.program_id(2)
    k4 = jnp.concatenate([kp_ref[...], kc_ref[...], kn_ref[...]], axis=0)
    v4 = jnp.concatenate([vp_ref[...], vc_ref[...], vn_ref[...]], axis=0)
    col = lax.broadcasted_iota(I32, (1, 3 * BLOCK), 1)
    sk = sink_ref[0]
    ones = jnp.ones((3 * BLOCK, HEAD_DIM), BF16)
    for u in range(2):
        q = q_ref[u * BLOCK:(u + 1) * BLOCK, :]
        q4 = jnp.concatenate([q[:, c * HEAD_DIM:(c + 1) * HEAD_DIM] for c in range(GQA_GROUP)], axis=0)
        k3 = k4[u * BLOCK:(u + 3) * BLOCK]
        v3 = v4[u * BLOCK:(u + 3) * BLOCK]
        s = lax.dot_general(q4, k3, (((1,), (1,)), ((), ())), preferred_element_type=F32)
        s = s * (HEAD_DIM ** -0.5) + tab_ref[0]
        if u == 0:
            lo, hi = jnp.where(i == 0, BLOCK, 0), 3 * BLOCK
        else:
            lo, hi = 0, jnp.where(i == npair - 1, 2 * BLOCK, 3 * BLOCK)
        s = jnp.where((col < lo) | (col >= hi), NEG_INF, s)
        m = jnp.maximum(jnp.max(s, axis=-1, keepdims=True), sk)
        p = jnp.exp(s - jnp.concatenate([m] * 3, axis=1))
        pv = jnp.dot(p.astype(BF16), jnp.concatenate([v3, ones], axis=1), preferred_element_type=F32)
        o = pv[:, :HEAD_DIM] / (pv[:, HEAD_DIM:] + jnp.exp(sk - m))
        o_ref[u * BLOCK:(u + 1) * BLOCK, :] = jnp.concatenate(
            [o[c * BLOCK:(c + 1) * BLOCK] for c in range(GQA_GROUP)], axis=1).astype(o_ref.dtype)


def _attn_a(h, tab_a, sink_col, bsz, t):
    nb = t // BLOCK
    npair = nb // 2
    n = bsz * t

    def edge_spec(cb, blk):
        return pl.BlockSpec((BLOCK, HEAD_DIM), lambda b, g, i: (b * nb + jnp.clip(blk(i), 0, nb - 1), cb + g))

    def pair_spec(cb):
        return pl.BlockSpec((2 * BLOCK, HEAD_DIM), lambda b, g, i: (b * npair + i, cb + g))

    return pl.pallas_call(
        functools.partial(_attn_a_kernel, npair=npair),
        grid=(bsz, N_KV_A, npair),
        in_specs=[pl.BlockSpec((2 * BLOCK, GQA_GROUP * HEAD_DIM), lambda b, g, i: (b * npair + i, g)),
                  edge_spec(KA_CB, lambda i: 2 * i - 1), pair_spec(KA_CB), edge_spec(KA_CB, lambda i: 2 * i + 2),
                  edge_spec(VA_CB, lambda i: 2 * i - 1), pair_spec(VA_CB), edge_spec(VA_CB, lambda i: 2 * i + 2),
                  pl.BlockSpec((1, GQA_GROUP * BLOCK, 3 * BLOCK), lambda b, g, i: (g, 0, 0)),
                  pl.BlockSpec((1, GQA_GROUP * BLOCK, LANES), lambda b, g, i: (g, 0, 0))],
        out_specs=pl.BlockSpec((2 * BLOCK, GQA_GROUP * HEAD_DIM), lambda b, g, i: (b * npair + i, g)),
        out_shape=jax.ShapeDtypeStruct((n, N_HEADS_A * HEAD_DIM), BF16),
        compiler_params=_cparams(("parallel", "parallel", "arbitrary")),
        name="attn_window",
    )(h, h, h, h, h, h, h, tab_a, sink_col)


def _attn_b_kernel(q_ref, k0_ref, k1_ref, k2_ref, v0_ref, v1_ref, v2_ref, tab_ref, o_ref):
    k3 = jnp.concatenate([k0_ref[...], k1_ref[...], k2_ref[...]], axis=0)
    v3 = jnp.concatenate([v0_ref[...], v1_ref[...], v2_ref[...]], axis=0)
    ones = jnp.ones((KWIN, HEAD_DIM), BF16)
    for u in range(2):
        cols = slice(u * HEAD_DIM, (u + 1) * HEAD_DIM)
        s = lax.dot_general(q_ref[:, cols], k3[:, cols], (((1,), (1,)), ((), ())), preferred_element_type=F32)
        s = s * (HEAD_DIM ** -0.5) + tab_ref[u, 0]
        p = jnp.exp(s - jnp.max(s, axis=-1, keepdims=True))
        pv = jnp.dot(p.astype(BF16), jnp.concatenate([v3[:, cols], ones], axis=1), preferred_element_type=F32)
        o_ref[:, cols] = (pv[:, :HEAD_DIM] / pv[:, HEAD_DIM:]).astype(o_ref.dtype)


def _attn_b(h, tab_b, bsz, t):
    nq = t // QT
    n = bsz * t

    def kv_spec(cb, j):
        return pl.BlockSpec((QT, 2 * HEAD_DIM),
                            lambda b, hp, g: (b * nq + jnp.clip(g - 1, 0, nq - 3) + j, cb // 2 + hp))

    def tab_map(b, hp, g):
        ty = jnp.where(g == 0, 0, jnp.where(g == nq - 1, 2, 1))
        return (hp, ty, 0, 0)

    return pl.pallas_call(
        _attn_b_kernel,
        grid=(bsz, N_HEADS_B // 2, nq),
        in_specs=[pl.BlockSpec((QT, 2 * HEAD_DIM), lambda b, hp, g: (b * nq + g, QB_CB // 2 + hp)),
                  kv_spec(KB_CB, 0), kv_spec(KB_CB, 1), kv_spec(KB_CB, 2),
                  kv_spec(VB_CB, 0), kv_spec(VB_CB, 1), kv_spec(VB_CB, 2),
                  pl.BlockSpec((2, 1, QT, KWIN), tab_map)],
        out_specs=pl.BlockSpec((QT, 2 * HEAD_DIM), lambda b, hp, g: (b * nq + g, hp)),
        out_shape=jax.ShapeDtypeStruct((n, N_HEADS_B * HEAD_DIM), BF16),
        compiler_params=_cparams(("parallel", "parallel", "arbitrary")),
        name="attn_nbr",
    )(h, h, h, h, h, h, h, tab_b)


def _attn_c_kernel(lam_ref, q_ref, k_ref, v_ref, tab_ref, g_ref, o_ref,
                   vaug_ref, m_ref, acc_ref, *, t, lam_init):
    i = pl.program_id(2)

    @pl.when(i == 0)
    def _():
        vaug_ref[:, :HEAD_DIM] = v_ref[...]
        vaug_ref[:, HEAD_DIM:] = jnp.ones((t, HEAD_DIM), BF16)

    q = q_ref[...] * jnp.asarray(DIFF_QK_DIM ** -0.5, BF16)
    qs = (q[:, :DIFF_QK_DIM], q[:, DIFF_QK_DIM:])
    m_ref[...] = jnp.full(m_ref.shape, NEG_INF, F32)
    acc_ref[...] = jnp.zeros(acc_ref.shape, F32)

    for j in range(t // KT):
        keys = slice(j * KT, (j + 1) * KT)
        bias = tab_ref[0, jnp.clip(2 * j - i + 3, 0, 5)]
        kt = k_ref[keys, :]
        for c in range(2):
            kc = kt[:, c * DIFF_QK_DIM:(c + 1) * DIFF_QK_DIM]
            s = lax.dot_general(qs[c], kc, (((1,), (1,)), ((), ())), preferred_element_type=F32) + bias
            m_old = m_ref[c]
            m_new = jnp.maximum(m_old, jnp.max(s, axis=-1, keepdims=True))
            a = jnp.exp(m_old - m_new)
            p = jnp.exp(s - jnp.concatenate([m_new] * (KT // LANES), axis=1))
            pv = jnp.dot(p.astype(BF16), vaug_ref[keys, :], preferred_element_type=F32)
            acc_ref[c] = jnp.concatenate([a, a], axis=1) * acc_ref[c] + pv
            m_ref[c] = m_new

    def normalised(c):
        acc = acc_ref[c]
        return acc[:, :HEAD_DIM] / acc[:, HEAD_DIM:]

    o = normalised(0) - lam_ref[0] * normalised(1)
    o = o * lax.rsqrt(jnp.mean(jnp.square(o), axis=-1, keepdims=True) + LN_EPS) * g_ref[...] * (1.0 - lam_init)
    o_ref[...] = o.astype(o_ref.dtype)


def _attn_c(h, tab_c, lam, subln_g, bsz, t, lam_init):
    nq = t // QT
    n = bsz * t
    grid_spec = pltpu.PrefetchScalarGridSpec(
        num_scalar_prefetch=1,
        grid=(bsz, N_HEADS_C, nq),
        in_specs=[pl.BlockSpec((QT, HEAD_DIM), lambda b, hd, i, s: (b * nq + i, QC_CB + hd)),
                  pl.BlockSpec((t, HEAD_DIM), lambda b, hd, i, s: (b, KC_CB + hd)),
                  pl.BlockSpec((t, HEAD_DIM), lambda b, hd, i, s: (b, VC_CB + hd)),
                  pl.BlockSpec((1, 6, QT, KT), lambda b, hd, i, s: (hd, 0, 0, 0)),
                  pl.BlockSpec((1, HEAD_DIM), lambda b, hd, i, s: (0, 0))],
        out_specs=pl.BlockSpec((QT, HEAD_DIM), lambda b, hd, i, s: (b * nq + i, hd)),
        scratch_shapes=[pltpu.VMEM((t, 2 * HEAD_DIM), BF16),
                        pltpu.VMEM((2, QT, LANES), F32),
                        pltpu.VMEM((2, QT, 2 * HEAD_DIM), F32)],
    )
    return pl.pallas_call(
        functools.partial(_attn_c_kernel, t=t, lam_init=lam_init),
        grid_spec=grid_spec,
        out_shape=jax.ShapeDtypeStruct((n, N_HEADS_C * HEAD_DIM), BF16),
        compiler_params=_cparams(("parallel", "parallel", "arbitrary")),
        name="attn_diff",
    )(lam, h, h, h, tab_c, subln_g)


def _split_bf16(x):
    hi = x.astype(BF16)
    lo = (x - hi.astype(F32)).astype(BF16)
    return hi, lo


def _out_ln_router_kernel(oa_ref, ob_ref, oc_ref, w0_ref, w1_ref, w2_ref, w3_ref, x_ref, g_ref, b_ref, wr_ref,
                          xe_ref, afft_ref, pre_ref, *, nj, tn):
    j = pl.program_id(1)
    q = w0_ref.shape[0]
    acc = jnp.dot(oa_ref[:, 0:q], w0_ref[...], preferred_element_type=F32)
    acc += jnp.dot(oa_ref[:, q:2 * q], w1_ref[...], preferred_element_type=F32)
    acc += jnp.dot(ob_ref[...], w2_ref[...], preferred_element_type=F32)
    acc += jnp.dot(oc_ref[...], w3_ref[...], preferred_element_type=F32)
    pre_ref[j] = ALPHA * x_ref[...] + acc

    @pl.when(j == nj - 1)
    def _():
        tm = pre_ref.shape[1]
        d = nj * tn
        tot = jnp.zeros((tm, 1), F32)
        for c in range(nj):
            tot += jnp.sum(pre_ref[c], axis=-1, keepdims=True)
        mu = tot / d
        sq = jnp.zeros((tm, 1), F32)
        for c in range(nj):
            sq += jnp.sum(jnp.square(pre_ref[c] - mu), axis=-1, keepdims=True)
        rstd = lax.rsqrt(sq / d + LN_EPS)
        logits = jnp.zeros((tm, LANES), F32)
        for c in range(nj):
            sl = slice(c * tn, (c + 1) * tn)
            y = (pre_ref[c] - mu) * rstd * g_ref[:, sl] + b_ref[:, sl]
            xe_ref[:, sl] = y
            yh, yl = _split_bf16(y)
            wh, wl = _split_bf16(wr_ref[sl, :])
            logits += jnp.dot(yh, wh, preferred_element_type=F32)
            logits += jnp.dot(yh, wl, preferred_element_type=F32)
            logits += jnp.dot(yl, wh, preferred_element_type=F32)
        lane = lax.broadcasted_iota(I32, (1, LANES), 1)
        logits = jnp.where(lane < N_EXPERTS, logits, NEG_INF)
        mx = jnp.max(logits, axis=-1, keepdims=True)
        ex = jnp.exp(logits - mx)
        aff = ex / jnp.sum(ex, axis=-1, keepdims=True)
        xe_ref[:, d:] = aff
        afft_ref[...] = aff.T[:N_EXPERTS, :]


def _out_ln_router(oa, ob, oc, w_out, l, x, g, b, w_router_pad):
    n, d = x.shape
    tm, tn = 512, 512
    nj = d // tn
    wq = w_out.shape[1] // 4
    assert oa.shape[1] == 2 * wq and ob.shape[1] == wq and oc.shape[1] == wq

    def w_spec(r):
        return pl.BlockSpec((None, wq, tn), lambda i, j: (l, r, j))

    return pl.pallas_call(
        functools.partial(_out_ln_router_kernel, nj=nj, tn=tn),
        grid=(n // tm, nj),
        in_specs=[pl.BlockSpec((tm, oa.shape[1]), lambda i, j: (i, 0)),
                  pl.BlockSpec((tm, ob.shape[1]), lambda i, j: (i, 0)),
                  pl.BlockSpec((tm, oc.shape[1]), lambda i, j: (i, 0)),
                  w_spec(0), w_spec(1), w_spec(2), w_spec(3),
                  pl.BlockSpec((tm, tn), lambda i, j: (i, j)),
                  pl.BlockSpec((1, d), lambda i, j: (0, 0)),
                  pl.BlockSpec((1, d), lambda i, j: (0, 0)),
                  pl.BlockSpec((d, LANES), lambda i, j: (0, 0))],
        out_specs=[pl.BlockSpec((tm, d + LANES), lambda i, j: (i, 0)),
                   pl.BlockSpec((N_EXPERTS, tm), lambda i, j: (0, i))],
        out_shape=[jax.ShapeDtypeStruct((n, d + LANES), F32),
                   jax.ShapeDtypeStruct((N_EXPERTS, n), F32)],
        scratch_shapes=[pltpu.VMEM((nj, tm, tn), F32)],
        compiler_params=_cparams(("parallel", "arbitrary")),
        name="out_proj_ln_router",
    )(oa, ob, oc, w_out, w_out, w_out, w_out, x, g, b, w_router_pad)


def _route_kernel(aff_ref, idx_ref, incl_ref, incl8_ref, *, cap):
    e_cnt, n = aff_ref.shape
    nch = n // LANES
    bits = pltpu.bitcast(aff_ref[...], I32)

    def count(mask):
        return jnp.sum(jnp.where(mask, 1.0, 0.0), axis=1, keepdims=True)

    def search(it, v):
        cand = v | jnp.left_shift(jnp.int32(1), 30 - it)
        return jnp.where(count(bits >= cand) >= cap, cand, v)

    thr = lax.fori_loop(0, 31, search, jnp.zeros((e_cnt, 1), I32))
    gtf = jnp.where(bits > thr, 1.0, 0.0)
    eqf = jnp.where(bits == thr, 1.0, 0.0)
    need = cap - jnp.sum(gtf, axis=1, keepdims=True)

    r = lax.broadcasted_iota(I32, (LANES, LANES), 0)
    c = lax.broadcasted_iota(I32, (LANES, LANES), 1)
    tri = jnp.where(r <= c, 1.0, 0.0).astype(BF16)
    ones = jnp.ones((LANES, LANES), BF16)

    def prefix(maskf):
        carry = jnp.zeros((e_cnt, LANES), F32)
        for j in range(nch):
            sl = slice(j * LANES, (j + 1) * LANES)
            mj = maskf[:, sl].astype(BF16)
            incl_ref[:, sl] = jnp.dot(mj, tri, preferred_element_type=F32) + carry
            carry = carry + jnp.dot(mj, ones, preferred_element_type=F32)

    prefix(eqf)
    tie_rank = incl_ref[...] - eqf
    chosen = gtf + eqf * jnp.where(tie_rank < need, 1.0, 0.0)
    prefix(chosen)
    for e in range(e_cnt):
        incl8_ref[e] = jnp.broadcast_to(incl_ref[e:e + 1, :], (8, n))

    ones_row = jnp.ones((8, LANES), BF16)
    slot_iota = lax.broadcasted_iota(I32, (LANES, 1), 0)

    def compact(it, carry):
        e = it // (cap // LANES)
        sb = it % (cap // LANES)
        slot = (slot_iota + sb * LANES).astype(F32)
        acc = jnp.zeros((LANES, LANES), F32)
        for j in range(nch):
            blk = incl8_ref[e, :, j * LANES:(j + 1) * LANES]
            acc += jnp.where(jnp.concatenate([blk] * (LANES // 8), axis=0) <= slot, 1.0, 0.0)
        tot = lax.dot_general(ones_row, acc.astype(BF16), (((1,), (1,)), ((), ())), preferred_element_type=F32)
        idx_ref[e, sb] = tot.astype(I32)
        return carry

    lax.fori_loop(0, e_cnt * (cap // LANES), compact, 0)


def _route(aff_t, cap):
    e_cnt, n = aff_t.shape
    idx = pl.pallas_call(
        functools.partial(_route_kernel, cap=cap),
        out_shape=jax.ShapeDtypeStruct((e_cnt, cap // LANES, 8, LANES), I32),
        scratch_shapes=[pltpu.VMEM((e_cnt, n), F32), pltpu.VMEM((e_cnt, 8, n), F32)],
        compiler_params=pltpu.CompilerParams(vmem_limit_bytes=VMEM_LIMIT),
        name="route",
    )(aff_t)
    return idx[:, :, 0, :].reshape(e_cnt, 1, cap)


def _ffn_kernel(idx_ref, idxn_ref, x_hbm, *refs, cap, nf, tf, d, rps, ne):
    wgu_refs, wd_refs = refs[0:4], refs[4:4 + W_STREAMS]
    y_ref, xe_ref, h_ref, gate_ref, stage_ref, sem = refs[4 + W_STREAMS:]
    e = pl.program_id(0)
    s = pl.program_id(1)
    nchunk = cap // rps
    g = e * nchunk + s
    cur = e % 2
    e_next = jnp.minimum(e + 1, ne - 1)
    lane = lax.broadcasted_iota(I32, (1, LANES), 1)

    def row_copy(row, slot, r):
        return pltpu.make_async_copy(x_hbm.at[pl.ds(row, 1), :], stage_ref.at[slot, pl.ds(r, 1), :], sem.at[slot])

    def start_rows(idx_r, chunk, slot):
        def start(r, carry):
            row_copy(idx_r[0, 0, chunk * rps + r], slot, r).start()
            return carry
        lax.fori_loop(0, rps, start, 0)

    def chunk_wait(slot):
        pltpu.make_async_copy(x_hbm.at[pl.ds(0, rps), :], stage_ref.at[slot], sem.at[slot]).wait()

    def land(chunk, slot, buf, expert):
        chunk_wait(slot)
        rows = pl.ds(pl.multiple_of(chunk * rps, rps), rps)
        xe_ref[buf, rows, :] = stage_ref[slot, :, 0:d].astype(BF16)
        aff = stage_ref[slot, :, d:d + LANES]
        gate_ref[buf, rows, :] = jnp.sum(jnp.where(lane == expert, aff, 0.0), axis=-1, keepdims=True)

    @pl.when((e == 0) & (s == 0))
    def _():
        def fetch(c, carry):
            start_rows(idx_ref, c, 0)
            land(c, 0, 0, 0)
            return carry
        lax.fori_loop(0, nchunk - 2, fetch, 0)
        start_rows(idx_ref, nchunk - 2, 1)
        start_rows(idx_ref, nchunk - 1, 2)

    def step_gather():
        @pl.when(s == 0)
        def _():
            land(nchunk - 2, (g + 1) % 3, cur, e)
            land(nchunk - 1, (g + 2) % 3, cur, e)

        @pl.when(s >= 2)
        def _():
            land(s - 2, (g + 1) % 3, 1 - cur, e_next)

        slot = g % 3
        for r in range(rps):
            row_copy(idxn_ref[0, 0, s * rps + r], slot, r).start()

    @pl.when(s < nf)
    def _():
        step_gather()
        w = jnp.concatenate([r[...].astype(BF16) for r in wgu_refs], axis=1)
        gu = jnp.dot(xe_ref[cur], w, preferred_element_type=F32)
        gt, u = gu[:, :tf], gu[:, tf:]
        hv = (gt * (1.0 / (1.0 + jnp.exp(-gt))) * u).astype(BF16)
        for c in range(nf):
            @pl.when(s == c)
            def _():
                h_ref[:, c * tf:(c + 1) * tf] = hv

    @pl.when(s >= nf)
    def _():
        step_gather()
        w = jnp.concatenate([r[...].astype(BF16) for r in wd_refs], axis=0)
        y = jnp.dot(h_ref[...], w, preferred_element_type=F32)
        y_ref[...] = y * gate_ref[cur]

    @pl.when((e == ne - 1) & (s == nchunk - 1))
    def _():
        last = ne * nchunk - 1
        chunk_wait((last - 1) % 3)
        chunk_wait(last % 3)


def _expert_ffn(idx, x_ext, w_gate, w_up, w_down, l, cap):
    _, e_cnt, d, f = w_gate.shape
    tf, tn = 256, 512
    nf, nn = f // tf, d // tn
    rps = cap // (nf + nn)
    assert rps * (nf + nn) == cap and rps % 16 == 0 and (nf + nn) % 2 == 0

    def gu_spec(c):
        return pl.BlockSpec((None, None, d, tf // 2), lambda e, s: (l, e, 0, 2 * jnp.minimum(s, nf - 1) + c))

    def d_spec(r):
        return pl.BlockSpec((None, None, f // W_STREAMS, tn), lambda e, s: (l, e, r, jnp.maximum(s - nf, 0)))

    return pl.pallas_call(
        functools.partial(_ffn_kernel, cap=cap, nf=nf, tf=tf, d=d, rps=rps, ne=e_cnt),
        grid=(e_cnt, nf + nn),
        in_specs=[pl.BlockSpec((1, 1, cap), lambda e, s: (e, 0, 0), memory_space=pltpu.SMEM),
                  pl.BlockSpec((1, 1, cap), lambda e, s: (jnp.minimum(e + 1, e_cnt - 1), 0, 0),
                               memory_space=pltpu.SMEM),
                  pl.BlockSpec(memory_space=pl.ANY)]
                 + [gu_spec(0), gu_spec(1)] * 2 + [d_spec(r) for r in range(W_STREAMS)],
        out_specs=pl.BlockSpec((cap, tn), lambda e, s: (e, jnp.maximum(s - nf, 0))),
        out_shape=jax.ShapeDtypeStruct((e_cnt * cap, d), F32),
        scratch_shapes=[pltpu.VMEM((2, cap, d), BF16),
                        pltpu.VMEM((cap, f), BF16),
                        pltpu.VMEM((2, cap, 1), F32),
                        pltpu.VMEM((3, rps, x_ext.shape[1]), F32),
                        pltpu.SemaphoreType.DMA((3,))],
        compiler_params=_cparams(("arbitrary", "arbitrary")),
        name="expert_ffn",
    )(idx, idx, x_ext, w_gate, w_gate, w_up, w_up, *([w_down] * W_STREAMS))


def _combine_kernel(idx_ref, y_hbm, acc_in_hbm, acc_hbm, buf_ref, ybuf_ref, gsem, ssem, ysem, *, cap):
    del acc_in_hbm
    e = pl.program_id(0)
    nchunk = cap // GATHER_ROWS

    def y_copy(c, slot):
        return pltpu.make_async_copy(y_hbm.at[pl.ds(e * cap + c * GATHER_ROWS, GATHER_ROWS), :],
                                     ybuf_ref.at[slot], ysem.at[slot])

    def row_in(row, slot, r):
        return pltpu.make_async_copy(acc_hbm.at[pl.ds(row, 1), :], buf_ref.at[slot, pl.ds(r, 1), :], gsem.at[slot])

    def row_out(row, slot, r):
        return pltpu.make_async_copy(buf_ref.at[slot, pl.ds(r, 1), :], acc_hbm.at[pl.ds(row, 1), :], ssem.at[slot])

    def for_rows(fn):
        def body(r, carry):
            fn(r)
            return carry
        lax.fori_loop(0, GATHER_ROWS, body, 0, unroll=8)

    def start_gather(c, slot):
        y_copy(c, slot).start()
        for_rows(lambda r: row_in(idx_ref[0, 0, c * GATHER_ROWS + r], slot, r).start())

    def wait_gather(c, slot):
        y_copy(c, slot).wait()
        pltpu.make_async_copy(acc_hbm.at[pl.ds(0, GATHER_ROWS), :], buf_ref.at[slot], gsem.at[slot]).wait()

    def start_scatter(c, slot):
        for_rows(lambda r: row_out(idx_ref[0, 0, c * GATHER_ROWS + r], slot, r).start())

    def wait_scatter(slot):
        pltpu.make_async_copy(buf_ref.at[slot], acc_hbm.at[pl.ds(0, GATHER_ROWS), :], ssem.at[slot]).wait()

    start_gather(0, 0)
    for c in range(nchunk):
        slot = c % 2
        if c + 1 < nchunk:
            if c >= 1:
                wait_scatter(1 - slot)
            start_gather(c + 1, 1 - slot)
        wait_gather(c, slot)
        buf_ref[slot] = buf_ref[slot] + ybuf_ref[slot]
        start_scatter(c, slot)
    if nchunk >= 2:
        wait_scatter(nchunk % 2)
    wait_scatter((nchunk - 1) % 2)


def _combine(idx, y, acc, cap):
    e_cnt = idx.shape[0]
    n, d = acc.shape
    return pl.pallas_call(
        functools.partial(_combine_kernel, cap=cap),
        grid=(e_cnt,),
        in_specs=[pl.BlockSpec((1, 1, cap), lambda e: (e, 0, 0), memory_space=pltpu.SMEM),
                  pl.BlockSpec(memory_space=pl.ANY),
                  pl.BlockSpec(memory_space=pl.ANY)],
        out_specs=pl.BlockSpec(memory_space=pl.ANY),
        out_shape=jax.ShapeDtypeStruct((n, d), F32),
        input_output_aliases={2: 0},
        scratch_shapes=[pltpu.VMEM((2, GATHER_ROWS, d), F32),
                        pltpu.VMEM((2, GATHER_ROWS, d), F32),
                        pltpu.SemaphoreType.DMA((2,)),
                        pltpu.SemaphoreType.DMA((2,)),
                        pltpu.SemaphoreType.DMA((2,))],
        compiler_params=_cparams(("arbitrary",)),
        name="combine",
    )(idx, y, acc)


def _ln2_kernel(x_ref, s_ref, g_ref, b_ref, o_ref, ob_ref):
    pre = ALPHA * x_ref[...] + s_ref[...]
    mu = jnp.mean(pre, axis=-1, keepdims=True)
    var = jnp.mean(jnp.square(pre - mu), axis=-1, keepdims=True)
    y = (pre - mu) * lax.rsqrt(var + LN_EPS) * g_ref[...] + b_ref[...]
    o_ref[...] = y
    ob_ref[...] = y.astype(BF16)


def _ln2(x_ext, moe, g, b):
    n, d = moe.shape
    tm = 256
    return pl.pallas_call(
        _ln2_kernel,
        grid=(n // tm,),
        in_specs=[pl.BlockSpec((tm, d), lambda i: (i, 0)),
                  pl.BlockSpec((tm, d), lambda i: (i, 0)),
                  pl.BlockSpec((1, d), lambda i: (0, 0)),
                  pl.BlockSpec((1, d), lambda i: (0, 0))],
        out_specs=[pl.BlockSpec((tm, d), lambda i: (i, 0)),
                   pl.BlockSpec((tm, d), lambda i: (i, 0))],
        out_shape=[jax.ShapeDtypeStruct((n, d), F32), jax.ShapeDtypeStruct((n, d), BF16)],
        compiler_params=_cparams(("parallel",)),
        name="ln2",
    )(x_ext, moe, g, b)


def _layer(x, xb, bsz, t, l, lam_init, shared, p):
    n = bsz * t
    cap = EC_CAPACITY_FACTOR * n // N_EXPERTS
    h = _in_proj(xb, shared["w_in"], l)
    oa = _attn_a(h, shared["tab_a"], p["sink_col"], bsz, t)
    ob = _attn_b(h, p["tab_b"], bsz, t)
    oc = _attn_c(h, shared["tab_c"], p["lam"], p["subln"], bsz, t, lam_init)
    x_ext, aff_t = _out_ln_router(oa, ob, oc, shared["w_out"], l, x, p["ln1_g"], p["ln1_b"], p["w_router"])
    idx = _route(aff_t, cap)
    y = _expert_ffn(idx, x_ext, shared["w_gate"], shared["w_up"], shared["w_down"], l, cap)
    moe = _combine(idx, y, jnp.zeros((n, D_MODEL), F32), cap)
    return _ln2(x_ext, moe, p["ln2_g"], p["ln2_b"])


def _trunk(x3, shared, layers):
    bsz, t, d = x3.shape
    x = x3.reshape(bsz * t, d)
    xb = x.astype(BF16)
    for l in range(DEPTH):
        lam_init = 0.8 - 0.6 * math.exp(-0.3 * l)
        x, xb = _layer(x, xb, bsz, t, l, lam_init, shared, layers[l])
    return x.reshape(bsz, t, d)


def kernel(x_prompt, x_sample, w_in, w_out, ln1_g, ln1_b, ln2_g, ln2_b, sink_a, rpb_b, diff_lambda, subln_c,
           t5_table, w_router, w_gate, w_up, w_down):
    shared = dict(
        tab_a=_table_a(t5_table[:, :N_HEADS_A]), tab_c=_table_c(t5_table[:, N_HEADS_A:]),
        w_in=w_in.astype(BF16), w_out=w_out.astype(BF16), w_gate=w_gate, w_up=w_up, w_down=w_down)
    layers = []
    for l in range(DEPTH):
        lam_init = 0.8 - 0.6 * math.exp(-0.3 * l)
        lf = diff_lambda[l].astype(F32)
        lam = jnp.exp(jnp.sum(lf[0] * lf[1])) - jnp.exp(jnp.sum(lf[2] * lf[3])) + lam_init
        sink_col = jnp.broadcast_to(jnp.repeat(sink_a[l].astype(F32), BLOCK).reshape(N_KV_A, GQA_GROUP * BLOCK, 1),
                                    (N_KV_A, GQA_GROUP * BLOCK, LANES))
        layers.append(dict(
            ln1_g=ln1_g[l][None], ln1_b=ln1_b[l][None], ln2_g=ln2_g[l][None], ln2_b=ln2_b[l][None],
            sink_col=sink_col, tab_b=_table_b(rpb_b[l]), lam=lam[None].astype(F32),
            subln=subln_c[l][None].astype(F32),
            w_router=jnp.pad(w_router[l], ((0, 0), (0, LANES - N_EXPERTS)))))
    return (_trunk(x_prompt, shared, layers), _trunk(x_sample, shared, layers))
```

```python
import functools
import math

import numpy as np
import jax
import jax.numpy as jnp
from jax import lax
from jax.experimental import pallas as pl
from jax.experimental.pallas import tpu as pltpu

F32 = jnp.float32
BF16 = jnp.bfloat16
I32 = jnp.int32

D_MODEL = 4096
DEPTH = 4
HEAD_DIM = 128
N_HEADS_A = 16
N_KV_A = 4
GQA_GROUP = 4
WINDOW = 128
BLOCK = 128
N_HEADS_B = 8
GRID_W = 64
NA_ROWS = 8
NA_COLS = 16
N_HEADS_C = 8
DIFF_QK_DIM = 64
T5_BUCKETS = 32
T5_MAX_DIST = 128
N_EXPERTS = 16
EC_CAPACITY_FACTOR = 2
D_EXPERT = 2048
LN_EPS = 1e-5
NEG_INF = -1e30
ALPHA = (2 * DEPTH) ** 0.25
IN_COLS = 9216

QA_CB, KA_CB, VA_CB = 0, 16, 20
QB_CB, KB_CB, VB_CB = 24, 32, 40
QC_CB, KC_CB, VC_CB = 48, 56, 64

LANES = 128
EXT_COLS = D_MODEL + LANES
VMEM_LIMIT = 56 * 1024 * 1024

QT = 256
KWIN = 768
KT = 512
QB = 8
GATHER_ROWS = 256


def _cparams(sem):
    return pltpu.CompilerParams(dimension_semantics=sem, vmem_limit_bytes=VMEM_LIMIT)


W_STREAMS = 4


def _mm_kernel(x_ref, *refs):
    w_refs, o_ref = refs[:-1], refs[-1]
    x = x_ref[...]
    tn = w_refs[0].shape[1]
    for c, w_ref in enumerate(w_refs):
        o_ref[:, c * tn:(c + 1) * tn] = jnp.dot(x, w_ref[...], preferred_element_type=F32).astype(o_ref.dtype)


def _in_proj(xb, w, l):
    n, d = xb.shape
    cols = w.shape[2]
    tm, tn = 1024, 1024
    ts = tn // W_STREAMS

    def w_spec(c):
        return pl.BlockSpec((None, d, ts), lambda i, j: (l, 0, j * W_STREAMS + c))

    return pl.pallas_call(
        _mm_kernel,
        grid=(n // tm, cols // tn),
        in_specs=[pl.BlockSpec((tm, d), lambda i, j: (i, 0))] + [w_spec(c) for c in range(W_STREAMS)],
        out_specs=pl.BlockSpec((tm, tn), lambda i, j: (i, j)),
        out_shape=jax.ShapeDtypeStruct((n, cols), BF16),
        compiler_params=_cparams(("parallel", "arbitrary")),
        name="in_proj",
    )(xb, *([w] * W_STREAMS))


def _t5_bucket(rel):
    half = T5_BUCKETS // 2
    max_exact = half // 2
    n = jnp.abs(rel)
    n_f = jnp.maximum(n, 1).astype(F32)
    large = max_exact + (jnp.log(n_f / max_exact) / math.log(T5_MAX_DIST / max_exact) * (half - max_exact)).astype(I32)
    large = jnp.minimum(large, half - 1)
    return jnp.where(rel > 0, half, 0) + jnp.where(n < max_exact, n, large)


def _t5_lookup(table, rel):
    onehot = (_t5_bucket(rel)[..., None] == jnp.arange(T5_BUCKETS)).astype(F32)
    return jnp.einsum('...b,bh->h...', onehot, table.astype(F32), precision=lax.Precision.HIGHEST)


def _table_a(t5_tab_a):
    rel = jnp.arange(3 * BLOCK)[None, :] - BLOCK - jnp.arange(BLOCK)[:, None]
    bias = jnp.where((jnp.abs(rel) <= WINDOW)[None], _t5_lookup(t5_tab_a, rel), NEG_INF)
    return bias.reshape(N_KV_A, GQA_GROUP * BLOCK, 3 * BLOCK)


def _table_c(t5_tab_c):
    qq = jnp.arange(QT)[:, None]
    kk = jnp.arange(KT)[None, :]
    rel = jnp.stack([QT * (v - 3) + kk - qq for v in range(1, 5)])
    near = _t5_lookup(t5_tab_c, rel)
    far = _t5_lookup(t5_tab_c, jnp.array([-(KT + QT), KT + QT]))
    far = jnp.broadcast_to(far[:, :, None, None], (N_HEADS_C, 2, QT, KT))
    return jnp.concatenate([far[:, 0:1], near, far[:, 1:2]], axis=1)


def _table_b_static():
    rows_q, rows_k = QT // GRID_W, KWIN // GRID_W
    qc = np.arange(GRID_W)[:, None]
    kc = np.arange(GRID_W)[None, :]
    col_start = np.clip(qc - NA_COLS // 2, 0, GRID_W - NA_COLS)
    col_valid = (kc >= col_start) & (kc < col_start + NA_COLS)
    dc = np.clip(kc - qc, -(NA_COLS - 1), NA_COLS - 1) + NA_COLS - 1
    oh_c = (dc[None] == np.arange(2 * NA_COLS - 1)[:, None, None]).astype(np.float32)
    ty = np.arange(3)[:, None, None]
    ql = np.arange(rows_q)[None, :, None]
    kl = np.arange(rows_k)[None, None, :]
    dr = kl - rows_q * ty - ql + NA_ROWS - 1
    win_lo = np.where(ty == 0, 0, np.where(ty == 1, ql, rows_k - NA_ROWS))
    win = (kl >= win_lo) & (kl < win_lo + NA_ROWS)
    oh_r = ((dr[None] == np.arange(2 * NA_ROWS - 1)[:, None, None, None]) & win[None]).astype(np.float32)
    valid = win[:, :, None, :, None] & col_valid[None, None, :, None, :]
    return oh_c, oh_r, valid.reshape(3, QT, KWIN)


_TB_OH_C, _TB_OH_R, _TB_VALID = _table_b_static()


def _table_b(rpb):
    hp = lax.Precision.HIGHEST
    cols = jnp.einsum('hrc,cqk->hrqk', rpb.astype(F32), _TB_OH_C, precision=hp)
    bias = jnp.einsum('hrqk,rtal->htaqlk', cols, _TB_OH_R, precision=hp).reshape(N_HEADS_B, 3, QT, KWIN)
    return jnp.where(_TB_VALID[None], bias, NEG_INF)


def _attn_a_kernel(q_ref, kp_ref, kc_ref, kn_ref, vp_ref, vc_ref, vn_ref, tab_ref, sink_ref, o_ref, *, npair):
    i = pl.program_id(2)
    k4 = jnp.concatenate([kp_ref[...], kc_ref[...], kn_ref[...]], axis=0)
    v4 = jnp.concatenate([vp_ref[...], vc_ref[...], vn_ref[...]], axis=0)
    col = lax.broadcasted_iota(I32, (1, 3 * BLOCK), 1)
    sk = sink_ref[0]
    ones = jnp.ones((3 * BLOCK, HEAD_DIM), BF16)
    for u in range(QB):
        q = q_ref[u * BLOCK:(u + 1) * BLOCK, :]
        q4 = jnp.concatenate([q[:, c * HEAD_DIM:(c + 1) * HEAD_DIM] for c in range(GQA_GROUP)], axis=0)
        k3 = k4[u * BLOCK:(u + 3) * BLOCK]
        v3 = v4[u * BLOCK:(u + 3) * BLOCK]
        s = lax.dot_general(q4, k3, (((1,), (1,)), ((), ())), preferred_element_type=F32)
        s = s * (HEAD_DIM ** -0.5) + tab_ref[0]
        lo = jnp.where(i == 0, BLOCK, 0) if u == 0 else 0
        hi = jnp.where(i == npair - 1, 2 * BLOCK, 3 * BLOCK) if u == QB - 1 else 3 * BLOCK
        s = jnp.where((col < lo) | (col >= hi), NEG_INF, s)
        m = jnp.maximum(jnp.max(s, axis=-1, keepdims=True), sk)
        p = jnp.exp(s - jnp.concatenate([m] * 3, axis=1))
        pv = jnp.dot(p.astype(BF16), jnp.concatenate([v3, ones], axis=1), preferred_element_type=F32)
        o = pv[:, :HEAD_DIM] / (pv[:, HEAD_DIM:] + jnp.exp(sk - m))
        o_ref[u * BLOCK:(u + 1) * BLOCK, :] = jnp.concatenate(
            [o[c * BLOCK:(c + 1) * BLOCK] for c in range(GQA_GROUP)], axis=1).astype(o_ref.dtype)


def _attn_a(h, tab_a, sink_col, bsz, t):
    nb = t // BLOCK
    npair = nb // QB
    n = bsz * t

    def edge_spec(cb, blk):
        return pl.BlockSpec((BLOCK, HEAD_DIM), lambda b, g, i: (b * nb + jnp.clip(blk(i), 0, nb - 1), cb + g))

    def pair_spec(cb):
        return pl.BlockSpec((QB * BLOCK, HEAD_DIM), lambda b, g, i: (b * npair + i, cb + g))

    return pl.pallas_call(
        functools.partial(_attn_a_kernel, npair=npair),
        grid=(bsz, N_KV_A, npair),
        in_specs=[pl.BlockSpec((QB * BLOCK, GQA_GROUP * HEAD_DIM), lambda b, g, i: (b * npair + i, g)),
                  edge_spec(KA_CB, lambda i: QB * i - 1), pair_spec(KA_CB), edge_spec(KA_CB, lambda i: QB * i + QB),
                  edge_spec(VA_CB, lambda i: QB * i - 1), pair_spec(VA_CB), edge_spec(VA_CB, lambda i: QB * i + QB),
                  pl.BlockSpec((1, GQA_GROUP * BLOCK, 3 * BLOCK), lambda b, g, i: (g, 0, 0)),
                  pl.BlockSpec((1, GQA_GROUP * BLOCK, LANES), lambda b, g, i: (g, 0, 0))],
        out_specs=pl.BlockSpec((QB * BLOCK, GQA_GROUP * HEAD_DIM), lambda b, g, i: (b * npair + i, g)),
        out_shape=jax.ShapeDtypeStruct((n, N_HEADS_A * HEAD_DIM), BF16),
        compiler_params=_cparams(("parallel", "parallel", "arbitrary")),
        name="attn_window",
    )(h, h, h, h, h, h, h, tab_a, sink_col)


def _attn_b_kernel(q_ref, k0_ref, k1_ref, k2_ref, v0_ref, v1_ref, v2_ref, tab_ref, o_ref):
    k3 = jnp.concatenate([k0_ref[...], k1_ref[...], k2_ref[...]], axis=0)
    v3 = jnp.concatenate([v0_ref[...], v1_ref[...], v2_ref[...]], axis=0)
    ones = jnp.ones((KWIN, HEAD_DIM), BF16)
    for u in range(2):
        cols = slice(u * HEAD_DIM, (u + 1) * HEAD_DIM)
        s = lax.dot_general(q_ref[:, cols], k3[:, cols], (((1,), (1,)), ((), ())), preferred_element_type=F32)
        s = s * (HEAD_DIM ** -0.5) + tab_ref[u, 0]
        p = jnp.exp(s - jnp.max(s, axis=-1, keepdims=True))
        pv = jnp.dot(p.astype(BF16), jnp.concatenate([v3[:, cols], ones], axis=1), preferred_element_type=F32)
        o_ref[:, cols] = (pv[:, :HEAD_DIM] / pv[:, HEAD_DIM:]).astype(o_ref.dtype)


def _attn_b(h, tab_b, bsz, t):
    nq = t // QT
    n = bsz * t

    def kv_spec(cb, j):
        return pl.BlockSpec((QT, 2 * HEAD_DIM),
                            lambda b, hp, g: (b * nq + jnp.clip(g - 1, 0, nq - 3) + j, cb // 2 + hp))

    def tab_map(b, hp, g):
        ty = jnp.where(g == 0, 0, jnp.where(g == nq - 1, 2, 1))
        return (hp, ty, 0, 0)

    return pl.pallas_call(
        _attn_b_kernel,
        grid=(bsz, N_HEADS_B // 2, nq),
        in_specs=[pl.BlockSpec((QT, 2 * HEAD_DIM), lambda b, hp, g: (b * nq + g, QB_CB // 2 + hp)),
                  kv_spec(KB_CB, 0), kv_spec(KB_CB, 1), kv_spec(KB_CB, 2),
                  kv_spec(VB_CB, 0), kv_spec(VB_CB, 1), kv_spec(VB_CB, 2),
                  pl.BlockSpec((2, 1, QT, KWIN), tab_map)],
        out_specs=pl.BlockSpec((QT, 2 * HEAD_DIM), lambda b, hp, g: (b * nq + g, hp)),
        out_shape=jax.ShapeDtypeStruct((n, N_HEADS_B * HEAD_DIM), BF16),
        compiler_params=_cparams(("parallel", "parallel", "arbitrary")),
        name="attn_nbr",
    )(h, h, h, h, h, h, h, tab_b)


def _attn_c_kernel(lam_ref, q_ref, k_ref, v_ref, tab_ref, g_ref, o_ref,
                   vaug_ref, m_ref, acc_ref, *, t, lam_init):
    i = pl.program_id(2)

    @pl.when(i == 0)
    def _():
        vaug_ref[:, :HEAD_DIM] = v_ref[...]
        vaug_ref[:, HEAD_DIM:] = jnp.ones((t, HEAD_DIM), BF16)

    q = q_ref[...] * jnp.asarray(DIFF_QK_DIM ** -0.5, BF16)
    qs = (q[:, :DIFF_QK_DIM], q[:, DIFF_QK_DIM:])
    m_ref[...] = jnp.full(m_ref.shape, NEG_INF, F32)
    acc_ref[...] = jnp.zeros(acc_ref.shape, F32)

    for j in range(t // KT):
        keys = slice(j * KT, (j + 1) * KT)
        bias = tab_ref[0, jnp.clip(2 * j - i + 3, 0, 5)]
        kt = k_ref[keys, :]
        for c in range(2):
            kc = kt[:, c * DIFF_QK_DIM:(c + 1) * DIFF_QK_DIM]
            s = lax.dot_general(qs[c], kc, (((1,), (1,)), ((), ())), preferred_element_type=F32) + bias
            m_old = m_ref[c]
            m_new = jnp.maximum(m_old, jnp.max(s, axis=-1, keepdims=True))
            a = jnp.exp(m_old - m_new)
            p = jnp.exp(s - jnp.concatenate([m_new] * (KT // LANES), axis=1))
            pv = jnp.dot(p.astype(BF16), vaug_ref[keys, :], preferred_element_type=F32)
            acc_ref[c] = jnp.concatenate([a, a], axis=1) * acc_ref[c] + pv
            m_ref[c] = m_new

    def normalised(c):
        acc = acc_ref[c]
        return acc[:, :HEAD_DIM] / acc[:, HEAD_DIM:]

    o = normalised(0) - lam_ref[0] * normalised(1)
    o = o * lax.rsqrt(jnp.mean(jnp.square(o), axis=-1, keepdims=True) + LN_EPS) * g_ref[...] * (1.0 - lam_init)
    o_ref[...] = o.astype(o_ref.dtype)


def _attn_c(h, tab_c, lam, subln_g, bsz, t, lam_init):
    nq = t // QT
    n = bsz * t
    grid_spec = pltpu.PrefetchScalarGridSpec(
        num_scalar_prefetch=1,
        grid=(bsz, N_HEADS_C, nq),
        in_specs=[pl.BlockSpec((QT, HEAD_DIM), lambda b, hd, i, s: (b * nq + i, QC_CB + hd)),
                  pl.BlockSpec((t, HEAD_DIM), lambda b, hd, i, s: (b, KC_CB + hd)),
                  pl.BlockSpec((t, HEAD_DIM), lambda b, hd, i, s: (b, VC_CB + hd)),
                  pl.BlockSpec((1, 6, QT, KT), lambda b, hd, i, s: (hd, 0, 0, 0)),
                  pl.BlockSpec((1, HEAD_DIM), lambda b, hd, i, s: (0, 0))],
        out_specs=pl.BlockSpec((QT, HEAD_DIM), lambda b, hd, i, s: (b * nq + i, hd)),
        scratch_shapes=[pltpu.VMEM((t, 2 * HEAD_DIM), BF16),
                        pltpu.VMEM((2, QT, LANES), F32),
                        pltpu.VMEM((2, QT, 2 * HEAD_DIM), F32)],
    )
    return pl.pallas_call(
        functools.partial(_attn_c_kernel, t=t, lam_init=lam_init),
        grid_spec=grid_spec,
        out_shape=jax.ShapeDtypeStruct((n, N_HEADS_C * HEAD_DIM), BF16),
        compiler_params=_cparams(("parallel", "parallel", "arbitrary")),
        name="attn_diff",
    )(lam, h, h, h, tab_c, subln_g)


def _split_bf16(x):
    hi = x.astype(BF16)
    lo = (x - hi.astype(F32)).astype(BF16)
    return hi, lo


def _out_ln_router_kernel(oa_ref, ob_ref, oc_ref, w0_ref, w1_ref, w2_ref, w3_ref, x_ref, g_ref, b_ref, wr_ref,
                          xe_ref, afft_ref, pre_ref, *, nj, tn):
    j = pl.program_id(1)
    q = w0_ref.shape[0]
    acc = jnp.dot(oa_ref[:, 0:q], w0_ref[...], preferred_element_type=F32)
    acc += jnp.dot(oa_ref[:, q:2 * q], w1_ref[...], preferred_element_type=F32)
    acc += jnp.dot(ob_ref[...], w2_ref[...], preferred_element_type=F32)
    acc += jnp.dot(oc_ref[...], w3_ref[...], preferred_element_type=F32)
    pre_ref[j] = ALPHA * x_ref[...] + acc

    @pl.when(j == nj - 1)
    def _():
        tm = pre_ref.shape[1]
        d = nj * tn
        tot = jnp.zeros((tm, 1), F32)
        for c in range(nj):
            tot += jnp.sum(pre_ref[c], axis=-1, keepdims=True)
        mu = tot / d
        sq = jnp.zeros((tm, 1), F32)
        for c in range(nj):
            sq += jnp.sum(jnp.square(pre_ref[c] - mu), axis=-1, keepdims=True)
        rstd = lax.rsqrt(sq / d + LN_EPS)
        logits = jnp.zeros((tm, LANES), F32)
        for c in range(nj):
            sl = slice(c * tn, (c + 1) * tn)
            y = (pre_ref[c] - mu) * rstd * g_ref[:, sl] + b_ref[:, sl]
            xe_ref[:, sl] = y
            yh, yl = _split_bf16(y)
            wh, wl = _split_bf16(wr_ref[sl, :])
            logits += jnp.dot(yh, wh, preferred_element_type=F32)
            logits += jnp.dot(yh, wl, preferred_element_type=F32)
            logits += jnp.dot(yl, wh, preferred_element_type=F32)
        lane = lax.broadcasted_iota(I32, (1, LANES), 1)
        logits = jnp.where(lane < N_EXPERTS, logits, NEG_INF)
        mx = jnp.max(logits, axis=-1, keepdims=True)
        ex = jnp.exp(logits - mx)
        aff = ex / jnp.sum(ex, axis=-1, keepdims=True)
        xe_ref[:, d:] = aff
        afft_ref[...] = aff.T[:N_EXPERTS, :]


def _out_ln_router(oa, ob, oc, w_out, l, x, g, b, w_router_pad):
    n, d = x.shape
    tm, tn = 512, 512
    nj = d // tn
    wq = w_out.shape[1] // 4
    assert oa.shape[1] == 2 * wq and ob.shape[1] == wq and oc.shape[1] == wq

    def w_spec(r):
        return pl.BlockSpec((None, wq, tn), lambda i, j: (l, r, j))

    return pl.pallas_call(
        functools.partial(_out_ln_router_kernel, nj=nj, tn=tn),
        grid=(n // tm, nj),
        in_specs=[pl.BlockSpec((tm, oa.shape[1]), lambda i, j: (i, 0)),
                  pl.BlockSpec((tm, ob.shape[1]), lambda i, j: (i, 0)),
                  pl.BlockSpec((tm, oc.shape[1]), lambda i, j: (i, 0)),
                  w_spec(0), w_spec(1), w_spec(2), w_spec(3),
                  pl.BlockSpec((tm, tn), lambda i, j: (i, j)),
                  pl.BlockSpec((1, d), lambda i, j: (0, 0)),
                  pl.BlockSpec((1, d), lambda i, j: (0, 0)),
                  pl.BlockSpec((d, LANES), lambda i, j: (0, 0))],
        out_specs=[pl.BlockSpec((tm, d + LANES), lambda i, j: (i, 0)),
                   pl.BlockSpec((N_EXPERTS, tm), lambda i, j: (0, i))],
        out_shape=[jax.ShapeDtypeStruct((n, d + LANES), F32),
                   jax.ShapeDtypeStruct((N_EXPERTS, n), F32)],
        scratch_shapes=[pltpu.VMEM((nj, tm, tn), F32)],
        compiler_params=_cparams(("parallel", "arbitrary")),
        name="out_proj_ln_router",
    )(oa, ob, oc, w_out, w_out, w_out, w_out, x, g, b, w_router_pad)


def _route_kernel(aff_ref, idx_ref, incl_ref, incl8_ref, *, cap):
    e_cnt, n = aff_ref.shape
    nch = n // LANES
    bits = pltpu.bitcast(aff_ref[...], I32)

    def count(mask):
        return jnp.sum(jnp.where(mask, 1.0, 0.0), axis=1, keepdims=True)

    def search(it, v):
        cand = v | jnp.left_shift(jnp.int32(1), 30 - it)
        return jnp.where(count(bits >= cand) >= cap, cand, v)

    thr = lax.fori_loop(0, 31, search, jnp.zeros((e_cnt, 1), I32))
    gtf = jnp.where(bits > thr, 1.0, 0.0)
    eqf = jnp.where(bits == thr, 1.0, 0.0)
    need = cap - jnp.sum(gtf, axis=1, keepdims=True)

    r = lax.broadcasted_iota(I32, (LANES, LANES), 0)
    c = lax.broadcasted_iota(I32, (LANES, LANES), 1)
    tri = jnp.where(r <= c, 1.0, 0.0).astype(BF16)
    ones = jnp.ones((LANES, LANES), BF16)

    def prefix(maskf):
        carry = jnp.zeros((e_cnt, LANES), F32)
        for j in range(nch):
            sl = slice(j * LANES, (j + 1) * LANES)
            mj = maskf[:, sl].astype(BF16)
            incl_ref[:, sl] = jnp.dot(mj, tri, preferred_element_type=F32) + carry
            carry = carry + jnp.dot(mj, ones, preferred_element_type=F32)

    prefix(eqf)
    tie_rank = incl_ref[...] - eqf
    chosen = gtf + eqf * jnp.where(tie_rank < need, 1.0, 0.0)
    prefix(chosen)
    for e in range(e_cnt):
        incl8_ref[e] = jnp.broadcast_to(incl_ref[e:e + 1, :], (8, n))

    ones_row = jnp.ones((8, LANES), BF16)
    slot_iota = lax.broadcasted_iota(I32, (LANES, 1), 0)

    def compact(it, carry):
        e = it // (cap // LANES)
        sb = it % (cap // LANES)
        slot = (slot_iota + sb * LANES).astype(F32)
        acc = jnp.zeros((LANES, LANES), F32)
        for j in range(nch):
            blk = incl8_ref[e, :, j * LANES:(j + 1) * LANES]
            acc += jnp.where(jnp.concatenate([blk] * (LANES // 8), axis=0) <= slot, 1.0, 0.0)
        tot = lax.dot_general(ones_row, acc.astype(BF16), (((1,), (1,)), ((), ())), preferred_element_type=F32)
        idx_ref[e, sb] = tot.astype(I32)
        return carry

    lax.fori_loop(0, e_cnt * (cap // LANES), compact, 0)


def _route(aff_t, cap):
    e_cnt, n = aff_t.shape
    idx = pl.pallas_call(
        functools.partial(_route_kernel, cap=cap),
        out_shape=jax.ShapeDtypeStruct((e_cnt, cap // LANES, 8, LANES), I32),
        scratch_shapes=[pltpu.VMEM((e_cnt, n), F32), pltpu.VMEM((e_cnt, 8, n), F32)],
        compiler_params=pltpu.CompilerParams(vmem_limit_bytes=VMEM_LIMIT),
        name="route",
    )(aff_t)
    return idx[:, :, 0, :].reshape(e_cnt, 1, cap)


def _ffn_kernel(idx_ref, idxn_ref, x_hbm, *refs, cap, nf, tf, d, rps, ne):
    wgu_refs, wd_refs = refs[0:4], refs[4:4 + W_STREAMS]
    y_ref, xe_ref, h_ref, gate_ref, stage_ref, sem = refs[4 + W_STREAMS:]
    e = pl.program_id(0)
    s = pl.program_id(1)
    nchunk = cap // rps
    g = e * nchunk + s
    cur = e % 2
    e_next = jnp.minimum(e + 1, ne - 1)
    lane = lax.broadcasted_iota(I32, (1, LANES), 1)

    def row_copy(row, slot, r):
        return pltpu.make_async_copy(x_hbm.at[pl.ds(row, 1), :], stage_ref.at[slot, pl.ds(r, 1), :], sem.at[slot])

    def start_rows(idx_r, chunk, slot):
        def start(r, carry):
            row_copy(idx_r[0, 0, chunk * rps + r], slot, r).start()
            return carry
        lax.fori_loop(0, rps, start, 0)

    def chunk_wait(slot):
        pltpu.make_async_copy(x_hbm.at[pl.ds(0, rps), :], stage_ref.at[slot], sem.at[slot]).wait()

    def land(chunk, slot, buf, expert):
        chunk_wait(slot)
        rows = pl.ds(pl.multiple_of(chunk * rps, rps), rps)
        xe_ref[buf, rows, :] = stage_ref[slot, :, 0:d].astype(BF16)
        aff = stage_ref[slot, :, d:d + LANES]
        gate_ref[buf, rows, :] = jnp.sum(jnp.where(lane == expert, aff, 0.0), axis=-1, keepdims=True)

    @pl.when((e == 0) & (s == 0))
    def _():
        def fetch(c, carry):
            start_rows(idx_ref, c, 0)
            land(c, 0, 0, 0)
            return carry
        lax.fori_loop(0, nchunk - 2, fetch, 0)
        start_rows(idx_ref, nchunk - 2, 1)
        start_rows(idx_ref, nchunk - 1, 2)

    def step_gather():
        @pl.when(s == 0)
        def _():
            land(nchunk - 2, (g + 1) % 3, cur, e)
            land(nchunk - 1, (g + 2) % 3, cur, e)

        @pl.when(s >= 2)
        def _():
            land(s - 2, (g + 1) % 3, 1 - cur, e_next)

        slot = g % 3
        for r in range(rps):
            row_copy(idxn_ref[0, 0, s * rps + r], slot, r).start()

    @pl.when(s < nf)
    def _():
        step_gather()
        w = jnp.concatenate([r[...].astype(BF16) for r in wgu_refs], axis=1)
        gu = jnp.dot(xe_ref[cur], w, preferred_element_type=F32)
        gt, u = gu[:, :tf], gu[:, tf:]
        hv = (gt * (1.0 / (1.0 + jnp.exp(-gt))) * u).astype(BF16)
        for c in range(nf):
            @pl.when(s == c)
            def _():
                h_ref[:, c * tf:(c + 1) * tf] = hv

    @pl.when(s >= nf)
    def _():
        step_gather()
        w = jnp.concatenate([r[...].astype(BF16) for r in wd_refs], axis=0)
        y = jnp.dot(h_ref[...], w, preferred_element_type=F32)
        y_ref[...] = y * gate_ref[cur]

    @pl.when((e == ne - 1) & (s == nchunk - 1))
    def _():
        last = ne * nchunk - 1
        chunk_wait((last - 1) % 3)
        chunk_wait(last % 3)


def _expert_ffn(idx, x_ext, w_gate, w_up, w_down, l, cap):
    _, e_cnt, d, f = w_gate.shape
    tf, tn = 256, 512
    nf, nn = f // tf, d // tn
    rps = cap // (nf + nn)
    assert rps * (nf + nn) == cap and rps % 16 == 0 and (nf + nn) % 2 == 0

    def gu_spec(c):
        return pl.BlockSpec((None, None, d, tf // 2), lambda e, s: (l, e, 0, 2 * jnp.minimum(s, nf - 1) + c))

    def d_spec(r):
        return pl.BlockSpec((None, None, f // W_STREAMS, tn), lambda e, s: (l, e, r, jnp.maximum(s - nf, 0)))

    return pl.pallas_call(
        functools.partial(_ffn_kernel, cap=cap, nf=nf, tf=tf, d=d, rps=rps, ne=e_cnt),
        grid=(e_cnt, nf + nn),
        in_specs=[pl.BlockSpec((1, 1, cap), lambda e, s: (e, 0, 0), memory_space=pltpu.SMEM),
                  pl.BlockSpec((1, 1, cap), lambda e, s: (jnp.minimum(e + 1, e_cnt - 1), 0, 0),
                               memory_space=pltpu.SMEM),
                  pl.BlockSpec(memory_space=pl.ANY)]
                 + [gu_spec(0), gu_spec(1)] * 2 + [d_spec(r) for r in range(W_STREAMS)],
        out_specs=pl.BlockSpec((cap, tn), lambda e, s: (e, jnp.maximum(s - nf, 0))),
        out_shape=jax.ShapeDtypeStruct((e_cnt * cap, d), F32),
        scratch_shapes=[pltpu.VMEM((2, cap, d), BF16),
                        pltpu.VMEM((cap, f), BF16),
                        pltpu.VMEM((2, cap, 1), F32),
                        pltpu.VMEM((3, rps, x_ext.shape[1]), F32),
                        pltpu.SemaphoreType.DMA((3,))],
        compiler_params=_cparams(("arbitrary", "arbitrary")),
        name="expert_ffn",
    )(idx, idx, x_ext, w_gate, w_gate, w_up, w_up, *([w_down] * W_STREAMS))


def _combine_kernel(idx_ref, y_hbm, acc_in_hbm, acc_hbm, buf_ref, ybuf_ref, gsem, ssem, ysem, *, cap):
    del acc_in_hbm
    e = pl.program_id(0)
    nchunk = cap // GATHER_ROWS

    def y_copy(c, slot):
        return pltpu.make_async_copy(y_hbm.at[pl.ds(e * cap + c * GATHER_ROWS, GATHER_ROWS), :],
                                     ybuf_ref.at[slot], ysem.at[slot])

    def row_in(row, slot, r):
        return pltpu.make_async_copy(acc_hbm.at[pl.ds(row, 1), :], buf_ref.at[slot, pl.ds(r, 1), :], gsem.at[slot])

    def row_out(row, slot, r):
        return pltpu.make_async_copy(buf_ref.at[slot, pl.ds(r, 1), :], acc_hbm.at[pl.ds(row, 1), :], ssem.at[slot])

    def for_rows(fn):
        def body(r, carry):
            fn(r)
            return carry
        lax.fori_loop(0, GATHER_ROWS, body, 0, unroll=8)

    def start_gather(c, slot):
        y_copy(c, slot).start()
        for_rows(lambda r: row_in(idx_ref[0, 0, c * GATHER_ROWS + r], slot, r).start())

    def wait_gather(c, slot):
        y_copy(c, slot).wait()
        pltpu.make_async_copy(acc_hbm.at[pl.ds(0, GATHER_ROWS), :], buf_ref.at[slot], gsem.at[slot]).wait()

    def start_scatter(c, slot):
        for_rows(lambda r: row_out(idx_ref[0, 0, c * GATHER_ROWS + r], slot, r).start())

    def wait_scatter(slot):
        pltpu.make_async_copy(buf_ref.at[slot], acc_hbm.at[pl.ds(0, GATHER_ROWS), :], ssem.at[slot]).wait()

    start_gather(0, 0)
    for c in range(nchunk):
        slot = c % 2
        if c + 1 < nchunk:
            if c >= 1:
                wait_scatter(1 - slot)
            start_gather(c + 1, 1 - slot)
        wait_gather(c, slot)
        buf_ref[slot] = buf_ref[slot] + ybuf_ref[slot]
        start_scatter(c, slot)
    if nchunk >= 2:
        wait_scatter(nchunk % 2)
    wait_scatter((nchunk - 1) % 2)


def _combine(idx, y, acc, cap):
    e_cnt = idx.shape[0]
    n, d = acc.shape
    return pl.pallas_call(
        functools.partial(_combine_kernel, cap=cap),
        grid=(e_cnt,),
        in_specs=[pl.BlockSpec((1, 1, cap), lambda e: (e, 0, 0), memory_space=pltpu.SMEM),
                  pl.BlockSpec(memory_space=pl.ANY),
                  pl.BlockSpec(memory_space=pl.ANY)],
        out_specs=pl.BlockSpec(memory_space=pl.ANY),
        out_shape=jax.ShapeDtypeStruct((n, d), F32),
        input_output_aliases={2: 0},
        scratch_shapes=[pltpu.VMEM((2, GATHER_ROWS, d), F32),
                        pltpu.VMEM((2, GATHER_ROWS, d), F32),
                        pltpu.SemaphoreType.DMA((2,)),
                        pltpu.SemaphoreType.DMA((2,)),
                        pltpu.SemaphoreType.DMA((2,))],
        compiler_params=_cparams(("arbitrary",)),
        name="combine",
    )(idx, y, acc)


def _ln2_kernel(x_ref, s_ref, g_ref, b_ref, o_ref, ob_ref):
    pre = ALPHA * x_ref[...] + s_ref[...]
    mu = jnp.mean(pre, axis=-1, keepdims=True)
    var = jnp.mean(jnp.square(pre - mu), axis=-1, keepdims=True)
    y = (pre - mu) * lax.rsqrt(var + LN_EPS) * g_ref[...] + b_ref[...]
    o_ref[...] = y
    ob_ref[...] = y.astype(BF16)


def _ln2(x_ext, moe, g, b):
    n, d = moe.shape
    tm = 256
    return pl.pallas_call(
        _ln2_kernel,
        grid=(n // tm,),
        in_specs=[pl.BlockSpec((tm, d), lambda i: (i, 0)),
                  pl.BlockSpec((tm, d), lambda i: (i, 0)),
                  pl.BlockSpec((1, d), lambda i: (0, 0)),
                  pl.BlockSpec((1, d), lambda i: (0, 0))],
        out_specs=[pl.BlockSpec((tm, d), lambda i: (i, 0)),
                   pl.BlockSpec((tm, d), lambda i: (i, 0))],
        out_shape=[jax.ShapeDtypeStruct((n, d), F32), jax.ShapeDtypeStruct((n, d), BF16)],
        compiler_params=_cparams(("parallel",)),
        name="ln2",
    )(x_ext, moe, g, b)


def _layer(x, xb, bsz, t, l, lam_init, shared, p):
    n = bsz * t
    cap = EC_CAPACITY_FACTOR * n // N_EXPERTS
    h = _in_proj(xb, shared["w_in"], l)
    oa = _attn_a(h, shared["tab_a"], p["sink_col"], bsz, t)
    ob = _attn_b(h, p["tab_b"], bsz, t)
    oc = _attn_c(h, shared["tab_c"], p["lam"], p["subln"], bsz, t, lam_init)
    x_ext, aff_t = _out_ln_router(oa, ob, oc, shared["w_out"], l, x, p["ln1_g"], p["ln1_b"], p["w_router"])
    idx = _route(aff_t, cap)
    y = _expert_ffn(idx, x_ext, shared["w_gate"], shared["w_up"], shared["w_down"], l, cap)
    moe = _combine(idx, y, jnp.zeros((n, D_MODEL), F32), cap)
    return _ln2(x_ext, moe, p["ln2_g"], p["ln2_b"])


def _trunk(x3, shared, layers):
    bsz, t, d = x3.shape
    x = x3.reshape(bsz * t, d)
    xb = x.astype(BF16)
    for l in range(DEPTH):
        lam_init = 0.8 - 0.6 * math.exp(-0.3 * l)
        x, xb = _layer(x, xb, bsz, t, l, lam_init, shared, layers[l])
    return x.reshape(bsz, t, d)


def kernel(x_prompt, x_sample, w_in, w_out, ln1_g, ln1_b, ln2_g, ln2_b, sink_a, rpb_b, diff_lambda, subln_c,
           t5_table, w_router, w_gate, w_up, w_down):
    shared = dict(
        tab_a=_table_a(t5_table[:, :N_HEADS_A]), tab_c=_table_c(t5_table[:, N_HEADS_A:]),
        w_in=w_in.astype(BF16), w_out=w_out.astype(BF16), w_gate=w_gate, w_up=w_up, w_down=w_down)
    layers = []
    for l in range(DEPTH):
        lam_init = 0.8 - 0.6 * math.exp(-0.3 * l)
        lf = diff_lambda[l].astype(F32)
        lam = jnp.exp(jnp.sum(lf[0] * lf[1])) - jnp.exp(jnp.sum(lf[2] * lf[3])) + lam_init
        sink_col = jnp.broadcast_to(jnp.repeat(sink_a[l].astype(F32), BLOCK).reshape(N_KV_A, GQA_GROUP * BLOCK, 1),
                                    (N_KV_A, GQA_GROUP * BLOCK, LANES))
        layers.append(dict(
            ln1_g=ln1_g[l][None], ln1_b=ln1_b[l][None], ln2_g=ln2_g[l][None], ln2_b=ln2_b[l][None],
            sink_col=sink_col, tab_b=_table_b(rpb_b[l]), lam=lam[None].astype(F32),
            subln=subln_c[l][None].astype(F32),
            w_router=jnp.pad(w_router[l], ((0, 0), (0, LANES - N_EXPERTS)))))
    return (_trunk(x_prompt, shared, layers), _trunk(x_sample, shared, layers))
```
